```python
import jax
import jax.numpy as jnp
from jax import lax
import numpy as np

D_MODEL = 1024
BATCH = 8
SEQ = 4096
DEPTH = 1

HEAD_DIM = 64
MIX_WIDTH = D_MODEL
NSA_HEADS = MIX_WIDTH // 2 // HEAD_DIM
NSA_KV_HEADS = 2
SWA_HEADS = MIX_WIDTH // 2 // HEAD_DIM
SWA_KV_HEADS = 2
CMP_BLOCK = 32
CMP_STRIDE = 16
CMP_HIDDEN = 256
SEL_BLOCK = 64
SEL_TOPK = 16
SEL_Q_CHUNK = 64
NSA_WINDOW = 512
SWA_WINDOW = 128
BAND_BLOCK = 128
N_NSA_BRANCHES = 3
D_FF = 2816
NORM_EPS = 1e-6
NEG_INF = -1e30
FORCE_SCORE = 1e9
ATTN_SCALE = HEAD_DIM ** -0.5
PROJ_SPLITS = (NSA_HEADS * HEAD_DIM, 2 * N_NSA_BRANCHES * NSA_KV_HEADS * HEAD_DIM, N_NSA_BRANCHES * NSA_HEADS, SWA_HEADS * HEAD_DIM, 2 * SWA_KV_HEADS * HEAD_DIM)
PROJ_WIDTH = sum(PROJ_SPLITS)

kernel_name = "hybrid_nsa_swa_sink_macaron_alibi"


def alibi_slopes(n_heads, n_groups):
    slopes = 2.0 ** (-8.0 * np.arange(1, n_heads + 1) / n_heads)
    return jnp.asarray(slopes.reshape(n_groups, n_heads // n_groups), jnp.float32)


def rmsnorm(x, g):
    xf = x.astype(jnp.float32)
    y = xf * lax.rsqrt(jnp.mean(xf * xf, axis=-1, keepdims=True) + NORM_EPS)
    return (y * g.astype(jnp.float32)).astype(x.dtype)


def swiglu(x, w_in, w_out):
    gate, up = jnp.split(x @ w_in, 2, axis=-1)
    return (jax.nn.silu(gate) * up) @ w_out


def banded_attention(q, k, v, slopes, window, sinks=None):
    B, T, G, R, D = q.shape
    nb = T // BAND_BLOCK
    n_prev = -(-(window - 1) // BAND_BLOCK)
    pad = n_prev * BAND_BLOCK
    kw_len = (n_prev + 1) * BAND_BLOCK
    kp = jnp.pad(k, ((0, 0), (pad, 0), (0, 0), (0, 0))).reshape(B, nb + n_prev, BAND_BLOCK, G, D)
    vp = jnp.pad(v, ((0, 0), (pad, 0), (0, 0), (0, 0))).reshape(B, nb + n_prev, BAND_BLOCK, G, D)
    kw = jnp.concatenate([kp[:, j:j + nb] for j in range(n_prev + 1)], axis=2)
    vw = jnp.concatenate([vp[:, j:j + nb] for j in range(n_prev + 1)], axis=2)
    qb = q.reshape(B, nb, BAND_BLOCK, G, R, D)
    s = jnp.einsum('bnqgrd,bnkgd->bngrqk', qb, kw) * ATTN_SCALE
    t_pos = np.arange(nb)[:, None] * BAND_BLOCK + np.arange(BAND_BLOCK)[None, :]
    s_pos = np.arange(nb)[:, None] * BAND_BLOCK - pad + np.arange(kw_len)[None, :]
    dist = t_pos[:, :, None] - s_pos[:, None, :]
    mask = jnp.asarray((dist >= 0) & (dist < window) & (s_pos[:, None, :] >= 0))
    bias = -slopes[None, :, :, None, None] * jnp.asarray(dist, jnp.float32)[:, None, None]
    s = jnp.where(mask[:, None, None], s + bias, NEG_INF)
    if sinks is not None:
        sink = jnp.broadcast_to(sinks[None, None, :, :, None, None], s.shape[:-1] + (1,))
        p = jax.nn.softmax(jnp.concatenate([s, sink], axis=-1), axis=-1)[..., :-1]
    else:
        p = jax.nn.softmax(s, axis=-1)
    o = jnp.einsum('bngrqk,bnkgd->bnqgrd', p, vw)
    return o.reshape(B, T, G, R, D)


def compress_blocks(kr, pos, w1, b1, w2):
    B, T, G, D = kr.shape
    nc = (T - CMP_BLOCK) // CMP_STRIDE + 1
    idx = np.arange(nc)[:, None] * CMP_STRIDE + np.arange(CMP_BLOCK)[None, :]
    blk = kr[:, idx] + pos[None, None, :, None, :]
    flat = blk.transpose(0, 1, 3, 2, 4).reshape(B, nc, G, CMP_BLOCK * D)
    return jax.nn.gelu(flat @ w1 + b1) @ w2


def selected_attention(q, k, v, sel_idx, slopes):
    B, T, G, R, D = q.shape
    ns = T // SEL_BLOCK
    n_sel = sel_idx.shape[-1]
    nq = T // SEL_Q_CHUNK
    kb = k.reshape(B, ns, SEL_BLOCK, G, D).transpose(0, 3, 1, 2, 4)
    vb = v.reshape(B, ns, SEL_BLOCK, G, D).transpose(0, 3, 1, 2, 4)
    qc = q.reshape(B, nq, SEL_Q_CHUNK, G, R, D).transpose(1, 0, 2, 3, 4, 5)
    ic = sel_idx.reshape(B, nq, SEL_Q_CHUNK, G, n_sel).transpose(1, 0, 2, 3, 4)
    tc = jnp.arange(T).reshape(nq, SEL_Q_CHUNK)
    b_ix = jnp.arange(B)[:, None, None, None]
    g_ix = jnp.arange(G)[None, None, :, None]

    def chunk(args):
        qx, ix, tx = args
        kg = kb[b_ix, g_ix, ix]
        vg = vb[b_ix, g_ix, ix]
        s = jnp.einsum('bcgrd,bcgkld->bcgrkl', qx, kg) * ATTN_SCALE
        spos = ix[..., None] * SEL_BLOCK + jnp.arange(SEL_BLOCK)
        dist = tx[None, :, None, None, None] - spos
        mask = (dist >= 0)[:, :, :, None]
        bias = -slopes[None, None, :, :, None, None] * dist.astype(jnp.float32)[:, :, :, None]
        s = jnp.where(mask, s + bias, NEG_INF)
        C = qx.shape[1]
        p = jax.nn.softmax(s.reshape(B, C, G, R, n_sel * SEL_BLOCK), axis=-1).reshape(s.shape)
        return jnp.einsum('bcgrkl,bcgkld->bcgrd', p, vg)

    o = lax.map(chunk, (qc, ic, tc))
    return o.transpose(1, 0, 2, 3, 4, 5).reshape(B, T, G, R, D)


def nsa_group(q, kc_raw, vc_raw, ks, vs, kw, vw, gate_logits, ck_pos, ck_w1, ck_b1, ck_w2, cv_pos, cv_w1, cv_b1, cv_w2, slopes):
    B, T, G, R, D = q.shape
    kc = compress_blocks(kc_raw, ck_pos, ck_w1, ck_b1, ck_w2).astype(jnp.float32)
    vc = compress_blocks(vc_raw, cv_pos, cv_w1, cv_b1, cv_w2).astype(jnp.float32)
    nc = kc.shape[1]
    blk_end = np.arange(nc) * CMP_STRIDE + CMP_BLOCK - 1
    dist_c = np.arange(T)[:, None] - blk_end[None, :]
    mask_c = jnp.asarray(dist_c >= 0)
    s = jnp.einsum('btgrd,bcgd->bgrtc', q, kc) * ATTN_SCALE
    s = jnp.where(mask_c, s - slopes[:, :, None, None] * jnp.asarray(dist_c, jnp.float32), NEG_INF)
    p_cmp = jnp.where(mask_c, jax.nn.softmax(s, axis=-1), 0.0)
    o_cmp = jnp.einsum('bgrtc,bcgd->btgrd', p_cmp, vc)
    ns = T // SEL_BLOCK
    c_start = np.arange(nc) * CMP_STRIDE
    s_start = np.arange(ns) * SEL_BLOCK
    overlap = np.clip(np.minimum(c_start[:, None] + CMP_BLOCK, s_start[None, :] + SEL_BLOCK) - np.maximum(c_start[:, None], s_start[None, :]), 0, None) / CMP_BLOCK
    p_slc = jnp.einsum('bgrtc,cs->btgs', p_cmp, jnp.asarray(overlap, jnp.float32))
    cur = np.arange(T) // SEL_BLOCK
    blk = np.arange(ns)
    valid = jnp.asarray(blk[None, :] <= cur[:, None])[None, :, None, :]
    forced = jnp.asarray((blk[None, :] == 0) | (blk[None, :] == cur[:, None]) | (blk[None, :] == cur[:, None] - 1))[None, :, None, :]
    score = jnp.where(forced, FORCE_SCORE, jnp.where(valid, p_slc, NEG_INF))
    _, sel_idx = lax.top_k(score, min(SEL_TOPK, ns))
    o_slc = selected_attention(q, ks, vs, sel_idx, slopes)
    o_win = banded_attention(q, kw, vw, slopes, NSA_WINDOW)
    g = jax.nn.sigmoid(gate_logits)
    return g[..., 0:1] * o_cmp + g[..., 1:2] * o_slc + g[..., 2:3] * o_win


def hybrid_mixer(xn, w_in, ck_pos, ck_w1, ck_b1, ck_w2, cv_pos, cv_w1, cv_b1, cv_w2, sinks, w_out):
    B, T, _ = xn.shape
    D = HEAD_DIM
    ga, ra = NSA_KV_HEADS, NSA_HEADS // NSA_KV_HEADS
    gb, rb = SWA_KV_HEADS, SWA_HEADS // SWA_KV_HEADS
    proj = (xn @ w_in).astype(jnp.float32)
    offsets = [int(o) for o in np.cumsum(PROJ_SPLITS)[:-1]]
    q_a, kv_a, gate_a, q_b, kv_b = jnp.split(proj, offsets, axis=-1)
    q_a = q_a.reshape(B, T, ga, ra, D)
    kc_raw, vc_raw, ks, vs, kw, vw = [t.reshape(B, T, ga, D) for t in jnp.split(kv_a, 2 * N_NSA_BRANCHES, axis=-1)]
    gate_a = gate_a.reshape(B, T, ga, ra, N_NSA_BRANCHES)
    o_a = nsa_group(q_a, kc_raw, vc_raw, ks, vs, kw, vw, gate_a, ck_pos, ck_w1, ck_b1, ck_w2, cv_pos, cv_w1, cv_b1, cv_w2, alibi_slopes(NSA_HEADS, NSA_KV_HEADS))
    q_b = q_b.reshape(B, T, gb, rb, D)
    k_b, v_b = [t.reshape(B, T, gb, D) for t in jnp.split(kv_b, 2, axis=-1)]
    o_b = banded_attention(q_b, k_b, v_b, alibi_slopes(SWA_HEADS, SWA_KV_HEADS), SWA_WINDOW, sinks.astype(jnp.float32).reshape(gb, rb))
    o = jnp.concatenate([o_a.reshape(B, T, NSA_HEADS * D), o_b.reshape(B, T, SWA_HEADS * D)], axis=-1)
    return o.astype(xn.dtype) @ w_out


def setup_inputs(seed: int = 0) -> dict:
    key = jax.random.key(seed)
    ks = jax.random.split(key, 24)
    L = DEPTH

    def nrm(k, shape, fan_in):
        return jax.random.normal(k, shape, jnp.float32) * fan_in ** -0.5

    def gain(k, shape):
        return 1.0 + 0.1 * jax.random.normal(k, shape, jnp.float32)

    return {
        'x': jax.random.normal(ks[0], (BATCH, SEQ, D_MODEL), jnp.float32),
        'ffn1_norm': gain(ks[1], (L, D_MODEL)),
        'ffn1_w_in': nrm(ks[2], (L, D_MODEL, 2 * D_FF), D_MODEL),
        'ffn1_w_out': nrm(ks[3], (L, D_FF, D_MODEL), D_FF),
        'mix_norm': gain(ks[4], (L, D_MODEL)),
        'w_mix_in': nrm(ks[5], (L, D_MODEL, PROJ_WIDTH), D_MODEL),
        'cmp_k_pos': 0.5 * jax.random.normal(ks[6], (L, CMP_BLOCK, HEAD_DIM), jnp.float32),
        'cmp_k_w1': nrm(ks[7], (L, CMP_BLOCK * HEAD_DIM, CMP_HIDDEN), CMP_BLOCK * HEAD_DIM),
        'cmp_k_b1': 0.02 * jax.random.normal(ks[8], (L, CMP_HIDDEN), jnp.float32),
        'cmp_k_w2': nrm(ks[9], (L, CMP_HIDDEN, HEAD_DIM), CMP_HIDDEN),
        'cmp_v_pos': 0.5 * jax.random.normal(ks[10], (L, CMP_BLOCK, HEAD_DIM), jnp.float32),
        'cmp_v_w1': nrm(ks[11], (L, CMP_BLOCK * HEAD_DIM, CMP_HIDDEN), CMP_BLOCK * HEAD_DIM),
        'cmp_v_b1': 0.02 * jax.random.normal(ks[12], (L, CMP_HIDDEN), jnp.float32),
        'cmp_v_w2': nrm(ks[13], (L, CMP_HIDDEN, HEAD_DIM), CMP_HIDDEN),
        'swa_sinks': 0.5 * jax.random.normal(ks[14], (L, SWA_HEADS), jnp.float32),
        'w_mix_out': nrm(ks[15], (L, MIX_WIDTH, D_MODEL), MIX_WIDTH),
        'ffn2_norm': gain(ks[16], (L, D_MODEL)),
        'ffn2_w_in': nrm(ks[17], (L, D_MODEL, 2 * D_FF), D_MODEL),
        'ffn2_w_out': nrm(ks[18], (L, D_FF, D_MODEL), D_FF),
        'final_norm': gain(ks[19], (D_MODEL,)),
    }


def reference(x, ffn1_norm, ffn1_w_in, ffn1_w_out, mix_norm, w_mix_in, cmp_k_pos, cmp_k_w1, cmp_k_b1, cmp_k_w2, cmp_v_pos, cmp_v_w1, cmp_v_b1, cmp_v_w2, swa_sinks, w_mix_out, ffn2_norm, ffn2_w_in, ffn2_w_out, final_norm):
    h = x
    for l in range(DEPTH):
        h = h + 0.5 * swiglu(rmsnorm(h, ffn1_norm[l]), ffn1_w_in[l], ffn1_w_out[l])
        h = h + hybrid_mixer(rmsnorm(h, mix_norm[l]), w_mix_in[l], cmp_k_pos[l], cmp_k_w1[l], cmp_k_b1[l], cmp_k_w2[l], cmp_v_pos[l], cmp_v_w1[l], cmp_v_b1[l], cmp_v_w2[l], swa_sinks[l], w_mix_out[l])
        h = h + 0.5 * swiglu(rmsnorm(h, ffn2_norm[l]), ffn2_w_in[l], ffn2_w_out[l])
    return rmsnorm(h, final_norm)
```

```python
import functools

import numpy as np
import jax
import jax.numpy as jnp
from jax import lax
from jax.experimental import pallas as pl
from jax.experimental.pallas import tpu as pltpu

F32 = jnp.float32
MXU_DTYPE = jnp.bfloat16

D_MODEL = 1024
HEAD_DIM = 64
N_GROUPS = 2
N_REP = 4
CMP_BLOCK = 32
CMP_STRIDE = 16
CMP_HIDDEN = 256
SEL_BLOCK = 64
SEL_TOPK = 16
NSA_WINDOW = 512
SWA_WINDOW = 128
D_FF = 2816
NORM_EPS = 1e-6
NEG_INF = -1e30
FORCE_SCORE = 1e9
ATTN_SCALE = HEAD_DIM ** -0.5
MASK_BIG = 2.0 ** 100
LANES = 128
AUG_K = 2 * LANES
_BLK_SHIFT = 6
_BLK_MASK = SEL_BLOCK - 1
VMEM_LIMIT = 48 * 1024 * 1024

_SLOPES = (2.0 ** (-8.0 * np.arange(1, 9) / 8)).reshape(N_GROUPS, N_REP)

_NT = (((1,), (1,)), ((), ()))


def _dot(a, b):
    return jnp.dot(a, b, preferred_element_type=F32)


def _dot_nt(a, b):
    return lax.dot_general(a, b, _NT, preferred_element_type=F32)


def _rmsnorm(x, g):
    ms = jnp.mean(x * x, axis=-1, keepdims=True)
    return x * lax.rsqrt(ms + NORM_EPS) * g


def _params(*sem):
    return pltpu.CompilerParams(dimension_semantics=sem, vmem_limit_bytes=VMEM_LIMIT)


def _ffn_kernel(x_ref, g_ref, wg_ref, wu_ref, wo_ref, *rest, n_f, final_norm):
    if final_norm:
        fg_ref, o_ref, xn_scr, acc_scr = rest
    else:
        o_ref, xn_scr, acc_scr = rest
    f = pl.program_id(1)

    @pl.when(f == 0)
    def _():
        xn_scr[...] = _rmsnorm(x_ref[...], g_ref[...]).astype(MXU_DTYPE)
        acc_scr[...] = jnp.zeros_like(acc_scr)

    xn = xn_scr[...]
    gate = _dot(xn, wg_ref[...])
    up = _dot(xn, wu_ref[...])
    act = gate * jax.nn.sigmoid(gate) * up
    acc_scr[...] += _dot(act.astype(MXU_DTYPE), wo_ref[...])

    @pl.when(f == n_f - 1)
    def _():
        h = x_ref[...] + 0.5 * acc_scr[...]
        if final_norm:
            h = _rmsnorm(h, fg_ref[...])
        o_ref[...] = h


def _ffn(x2, norm_g, w_in, w_out, final_g=None, *, tm=1024, tf=256):
    m = x2.shape[0]
    tm = min(tm, m)
    n_f = D_FF // tf
    final_norm = final_g is not None
    in_specs = [
        pl.BlockSpec((tm, D_MODEL), lambda i, f: (i, 0)),
        pl.BlockSpec((1, D_MODEL), lambda i, f: (0, 0)),
        pl.BlockSpec((D_MODEL, tf), lambda i, f: (0, f)),
        pl.BlockSpec((D_MODEL, tf), lambda i, f: (0, n_f + f)),
        pl.BlockSpec((tf, D_MODEL), lambda i, f: (f, 0)),
    ]
    args = [x2, norm_g.reshape(1, D_MODEL), w_in, w_in, w_out]
    if final_norm:
        in_specs.append(pl.BlockSpec((1, D_MODEL), lambda i, f: (0, 0)))
        args.append(final_g.reshape(1, D_MODEL))
    return pl.pallas_call(
        functools.partial(_ffn_kernel, n_f=n_f, final_norm=final_norm),
        grid=(m // tm, n_f),
        in_specs=in_specs,
        out_specs=pl.BlockSpec((tm, D_MODEL), lambda i, f: (i, 0)),
        out_shape=jax.ShapeDtypeStruct((m, D_MODEL), F32),
        scratch_shapes=[pltpu.VMEM((tm, D_MODEL), MXU_DTYPE), pltpu.VMEM((tm, D_MODEL), F32)],
        compiler_params=_params("arbitrary", "arbitrary"),
        name="ffn_final" if final_norm else "ffn",
    )(*args)


P16_W = 1024 + 6 * LANES
PROJ_W = P16_W + 3 * LANES


def _proj_kernel(h_ref, g_ref, w_ref, p16_ref, kc_ref, vc_ref, gt_ref):
    xn = _rmsnorm(h_ref[...], g_ref[...]).astype(MXU_DTYPE)
    y = _dot(xn, w_ref[...])
    p16_ref[:, 0:1024] = (y[:, 0:1024] * ATTN_SCALE).astype(MXU_DTYPE)
    p16_ref[:, 1024:P16_W] = y[:, 1024:P16_W].astype(MXU_DTYPE)
    kc_ref[...] = y[:, P16_W:P16_W + LANES]
    vc_ref[...] = y[:, P16_W + LANES:P16_W + 2 * LANES]
    gt_ref[...] = y[:, P16_W + 2 * LANES:PROJ_W]


def _proj(h2, norm_g, w, *, tm=512):
    m = h2.shape[0]
    tm = min(tm, m)
    row = lambda i: (i, 0)
    return pl.pallas_call(
        _proj_kernel,
        grid=(m // tm,),
        in_specs=[
            pl.BlockSpec((tm, D_MODEL), row),
            pl.BlockSpec((1, D_MODEL), lambda i: (0, 0)),
            pl.BlockSpec((D_MODEL, PROJ_W), lambda i: (0, 0)),
        ],
        out_specs=[
            pl.BlockSpec((tm, P16_W), row),
            pl.BlockSpec((tm, LANES), row),
            pl.BlockSpec((tm, LANES), row),
            pl.BlockSpec((tm, LANES), row),
        ],
        out_shape=[
            jax.ShapeDtypeStruct((m, P16_W), MXU_DTYPE),
            jax.ShapeDtypeStruct((m, LANES), F32),
            jax.ShapeDtypeStruct((m, LANES), F32),
            jax.ShapeDtypeStruct((m, LANES), F32),
        ],
        compiler_params=_params("arbitrary"),
        name="mix_proj",
    )(h2, norm_g.reshape(1, D_MODEL), w)


def _gelu_tanh(x):
    return 0.5 * x * (1.0 + jnp.tanh(np.sqrt(2.0 / np.pi) * (x + 0.044715 * (x * x * x))))


def _compress_kernel(raw_ref, pa_ref, pb_ref, wa_ref, wb_ref, b1_ref, w2_ref, o_ref):
    x = raw_ref[0]
    n_rows = x.shape[0]
    first = _dot((x + pa_ref[...]).astype(MXU_DTYPE), wa_ref[...])
    second = _dot((x + pb_ref[...]).astype(MXU_DTYPE), wb_ref[...])
    hid = first + pltpu.roll(second, n_rows - 1, axis=0) + b1_ref[...]
    out = _dot(_gelu_tanh(hid).astype(MXU_DTYPE), w2_ref[...])
    row = lax.broadcasted_iota(jnp.int32, out.shape, 0)
    o_ref[0] = jnp.where(row == n_rows - 1, 0.0, out).astype(o_ref.dtype)


def _compress(raw, pos, w1, b1, w2):
    b, t, _ = raw.shape
    n_rows = t // CMP_STRIDE
    half = CMP_STRIDE
    kw = half * LANES
    eye = jnp.eye(N_GROUPS, dtype=F32)
    w1r = w1.reshape(CMP_BLOCK, HEAD_DIM, CMP_HIDDEN)

    def expand(wpart):
        return jnp.einsum("jdh,ab->jadbh", wpart, eye).reshape(kw, N_GROUPS * CMP_HIDDEN).astype(MXU_DTYPE)

    def expand_pos(ppart):
        return jnp.broadcast_to(ppart[:, None, :], (half, N_GROUPS, HEAD_DIM)).reshape(1, kw)

    wa, wb = expand(w1r[:half]), expand(w1r[half:])
    pa, pb = expand_pos(pos[:half]), expand_pos(pos[half:])
    b1e = jnp.tile(b1.reshape(1, CMP_HIDDEN), (1, N_GROUPS))
    w2e = jnp.einsum("hd,ab->ahbd", w2, eye).reshape(N_GROUPS * CMP_HIDDEN, LANES).astype(MXU_DTYPE)
    const = lambda i: (0, 0)
    return pl.pallas_call(
        _compress_kernel,
        grid=(b,),
        in_specs=[
            pl.BlockSpec((1, n_rows, kw), lambda i: (i, 0, 0)),
            pl.BlockSpec((1, kw), const),
            pl.BlockSpec((1, kw), const),
            pl.BlockSpec((kw, N_GROUPS * CMP_HIDDEN), const),
            pl.BlockSpec((kw, N_GROUPS * CMP_HIDDEN), const),
            pl.BlockSpec((1, N_GROUPS * CMP_HIDDEN), const),
            pl.BlockSpec((N_GROUPS * CMP_HIDDEN, LANES), const),
        ],
        out_specs=pl.BlockSpec((1, n_rows, LANES), lambda i: (i, 0, 0)),
        out_shape=jax.ShapeDtypeStruct((b, n_rows, LANES), MXU_DTYPE),
        compiler_params=_params("arbitrary"),
        name="compress",
    )(raw.reshape(b, n_rows, kw), pa, pb, wa, wb, b1e, w2e)


def _lane_half(shape):
    return lax.broadcasted_iota(jnp.int32, shape, len(shape) - 1) >> _BLK_SHIFT


def _alibi_cols(c0, c1, c2, c3, shape):
    lane = lax.broadcasted_iota(jnp.int32, shape, 1)
    z = jnp.zeros(shape, F32)
    return jnp.where(lane == 0, c0, jnp.where(lane == 1, c1, jnp.where(lane == 2, c2, jnp.where(lane == 3, c3, z))))


def _q_alibi(tq, t0, slope):
    t = t0 + lax.broadcasted_iota(jnp.int32, (tq, LANES), 0)
    hi = (t >> _BLK_SHIFT).astype(F32)
    lo = (t & _BLK_MASK).astype(F32)
    return _alibi_cols(slope * SEL_BLOCK * hi, slope * lo, slope * SEL_BLOCK, slope, (tq, LANES)).astype(MXU_DTYPE)


def _k_alibi(pos):
    hi = (pos >> _BLK_SHIFT).astype(F32)
    lo = (pos & _BLK_MASK).astype(F32)
    return _alibi_cols(-1.0, -1.0, hi, lo, pos.shape).astype(MXU_DTYPE)


def _build_qaug(qaug_scr, q_ref, t0, tq, other):
    half = _lane_half((tq, LANES))
    for r in range(N_REP):
        slab = q_ref[0, :, r * LANES:(r + 1) * LANES]
        for g in range(N_GROUPS):
            fill = jnp.zeros_like(slab) if other is None else other
            qaug_scr[g, r * tq:(r + 1) * tq, 0:LANES] = jnp.where(half == g, slab, fill)
            qaug_scr[g, r * tq:(r + 1) * tq, LANES:AUG_K] = _q_alibi(tq, t0, float(_SLOPES[g, r]))


def _store_slabs(o_ref, per_group, tq):
    lane = lax.broadcasted_iota(jnp.int32, (tq, LANES), 1)
    for r in range(N_REP):
        a = per_group[0][r * tq:(r + 1) * tq]
        b = per_group[1][r * tq:(r + 1) * tq]
        o_ref[0, :, r * LANES:(r + 1) * LANES] = jnp.where(lane < HEAD_DIM, a, b).astype(o_ref.dtype)


def _cmp_kernel(q_ref, kc_ref, vc_ref, ovl_ref, o_ref, ns_ref, qaug_scr, *, tq, n_cmp, n_sel):
    i = pl.program_id(1)
    t0 = i * tq
    _build_qaug(qaug_scr, q_ref, t0, tq, None)

    kc = kc_ref[0]
    vc = vc_ref[0]
    half_k = _lane_half((n_cmp, LANES))
    c_pos = lax.broadcasted_iota(jnp.int32, (n_cmp, LANES), 0) * CMP_STRIDE + (CMP_BLOCK - 1)
    k_ali = _k_alibi(c_pos)
    t_row = t0 + lax.broadcasted_iota(jnp.int32, (tq, n_cmp), 0)
    c_end = lax.broadcasted_iota(jnp.int32, (tq, n_cmp), 1) * CMP_STRIDE + (CMP_BLOCK - 1)
    vis = (c_end <= t_row)[None]

    outs, psl_t = [], []
    for g in range(N_GROUPS):
        kaug = jnp.concatenate([jnp.where(half_k == g, kc, jnp.zeros_like(kc)), k_ali], axis=1)
        s = _dot_nt(qaug_scr[g], kaug).reshape(N_REP, tq, n_cmp)
        s = jnp.where(vis, s, NEG_INF)
        m = jnp.max(s, axis=-1, keepdims=True)
        e = jnp.where(vis, jnp.exp(s - m), 0.0)
        l = jnp.sum(e, axis=-1, keepdims=True)
        p = e * jnp.where(l > 0.0, 1.0 / l, 0.0)
        pb = p.astype(MXU_DTYPE).reshape(N_REP * tq, n_cmp)
        outs.append(_dot(pb, vc))
        pt = _dot_nt(ovl_ref[...], pb)
        psl_t.append(pt[:, 0:tq] + pt[:, tq:2 * tq] + pt[:, 2 * tq:3 * tq] + pt[:, 3 * tq:4 * tq])
    _store_slabs(o_ref, outs, tq)

    blk = lax.broadcasted_iota(jnp.int32, (SEL_BLOCK, tq), 0)
    cur = (t0 + lax.broadcasted_iota(jnp.int32, (SEL_BLOCK, tq), 1)) >> _BLK_SHIFT
    valid = blk <= cur
    forced = (blk == 0) | (blk == cur) | (blk == cur - 1)
    notsel = []
    for g in (1, 0):
        score = jnp.where(forced, FORCE_SCORE, jnp.where(valid, psl_t[g][0:SEL_BLOCK], NEG_INF))
        cnt = jnp.zeros((SEL_BLOCK, tq), jnp.int32)
        for j in range(n_sel):
            xj = jnp.broadcast_to(score[j:j + 1, :], (SEL_BLOCK, tq))
            beats = (xj > score) | ((xj == score) & (blk > j))
            cnt = cnt + jnp.where(beats, 1, 0)
        keep = (cnt < SEL_TOPK) & valid
        notsel.append(jnp.where(keep, 0.0, 1.0))
    ns_ref[0] = jnp.transpose(jnp.concatenate(notsel, axis=0)).astype(ns_ref.dtype)


def _overlap_t(t):
    nc_pad = t // CMP_STRIDE
    ns = t // SEL_BLOCK
    c_start = np.arange(nc_pad) * CMP_STRIDE
    s_start = np.arange(ns) * SEL_BLOCK
    ov = np.clip(np.minimum(c_start[:, None] + CMP_BLOCK, s_start[None, :] + SEL_BLOCK)
                 - np.maximum(c_start[:, None], s_start[None, :]), 0, None) / CMP_BLOCK
    ov[nc_pad - 1] = 0.0
    out = np.zeros((LANES, nc_pad), np.float32)
    out[0:ns] = ov.T
    out[HEAD_DIM:HEAD_DIM + ns] = ov.T
    return out


def _cmp_attention(p16, kc, vc, *, tq=256):
    b, t, _ = p16.shape
    tq = min(tq, t)
    n_cmp = t // CMP_STRIDE
    n_sel = t // SEL_BLOCK
    assert n_sel <= SEL_BLOCK and n_cmp % LANES == 0
    ovl = jnp.asarray(_overlap_t(t), MXU_DTYPE)
    return pl.pallas_call(
        functools.partial(_cmp_kernel, tq=tq, n_cmp=n_cmp, n_sel=n_sel),
        grid=(b, t // tq),
        in_specs=[
            pl.BlockSpec((1, tq, 512), lambda bi, i: (bi, i, 0)),
            pl.BlockSpec((1, n_cmp, LANES), lambda bi, i: (bi, 0, 0)),
            pl.BlockSpec((1, n_cmp, LANES), lambda bi, i: (bi, 0, 0)),
            pl.BlockSpec((LANES, n_cmp), lambda bi, i: (0, 0)),
        ],
        out_specs=[
            pl.BlockSpec((1, tq, 512), lambda bi, i: (bi, i, 0)),
            pl.BlockSpec((1, tq, LANES), lambda bi, i: (bi, i, 0)),
        ],
        out_shape=[
            jax.ShapeDtypeStruct((b, t, 512), F32),
            jax.ShapeDtypeStruct((b, t, LANES), MXU_DTYPE),
        ],
        scratch_shapes=[pltpu.VMEM((N_GROUPS, N_REP * tq, AUG_K), MXU_DTYPE)],
        compiler_params=_params("arbitrary", "arbitrary"),
        name="cmp_attn",
    )(p16, kc, vc, ovl)


def _attn_kernel(*refs, tile, seq, mode, n_prev, has_sink):
    refs = list(refs)
    sink_ref = refs.pop(0) if has_sink else None
    q_ref = refs.pop(0)
    ns_ref = refs.pop(0) if mode == "sel" else None
    k_ref, v_ref, o_ref, kaug_scr, vaug_scr, qaug_scr, acc_scr, m_scr = refs
    i = pl.program_id(1)
    t0 = i * tile
    rows = N_REP * tile
    build_chunk = min(512, seq)

    @pl.when(i == 0)
    def _():
        def body(c, carry):
            r0 = pl.multiple_of(c * build_chunk, build_chunk)
            k = k_ref[0, pl.ds(r0, build_chunk), :]
            v = v_ref[0, pl.ds(r0, build_chunk), :]
            half = _lane_half((build_chunk, LANES))
            pos = r0 + lax.broadcasted_iota(jnp.int32, (build_chunk, LANES), 0)
            k_ali = _k_alibi(pos)
            lane = lax.broadcasted_iota(jnp.int32, (build_chunk, LANES), 1)
            if mode == "sel":
                off = jnp.where((lane & _BLK_MASK) == (pos >> _BLK_SHIFT), -MASK_BIG, 0.0).astype(MXU_DTYPE)
            else:
                off = jnp.zeros((build_chunk, LANES), MXU_DTYPE)
            for g in range(N_GROUPS):
                kaug_scr[g, pl.ds(r0, build_chunk), 0:LANES] = jnp.where(half == g, k, off)
                kaug_scr[g, pl.ds(r0, build_chunk), LANES:AUG_K] = k_ali
                vaug_scr[g, pl.ds(r0, build_chunk), :] = jnp.where(half == g, v, jnp.ones_like(v))
            return carry
        lax.fori_loop(0, seq // build_chunk, body, 0)

    _build_qaug(qaug_scr, q_ref, t0, tile, ns_ref[0] if mode == "sel" else None)
    acc_scr[...] = jnp.zeros_like(acc_scr)
    m_scr[...] = jnp.full_like(m_scr, -3e38)

    def attend(k0, mask):
        row = lax.broadcasted_iota(jnp.int32, (tile, tile), 0)
        col = lax.broadcasted_iota(jnp.int32, (tile, tile), 1)
        for g in range(N_GROUPS):
            ka = kaug_scr[g, pl.ds(k0, tile), :]
            va = vaug_scr[g, pl.ds(k0, tile), :]
            s = _dot_nt(qaug_scr[g], ka)
            if mask == "causal":
                s = jnp.where((col <= row)[None], s.reshape(N_REP, tile, tile), NEG_INF).reshape(rows, tile)
            elif mask == "upper":
                s = jnp.where((col > row)[None], s.reshape(N_REP, tile, tile), NEG_INF).reshape(rows, tile)
            m_old = m_scr[g]
            m_new = jnp.maximum(m_old, jnp.max(s, axis=1, keepdims=True))
            alpha = jnp.exp(m_old - m_new)
            p = jnp.exp(s - jnp.tile(m_new, (1, tile // LANES)))
            acc_scr[g] = acc_scr[g] * alpha + _dot(p.astype(MXU_DTYPE), va)
            m_scr[g] = m_new

    if mode == "sel":
        def body(j, carry):
            attend(pl.multiple_of(j * tile, tile), None)
            return carry
        lax.fori_loop(0, i, body, 0)
    else:
        for d in range(n_prev, 0, -1):
            @pl.when(i >= d)
            def _(d=d):
                attend(pl.multiple_of((i - d) * tile, tile), "upper" if d == n_prev else None)
    attend(pl.multiple_of(t0, tile), "causal")

    outs = []
    half = _lane_half((rows, LANES))
    for g in range(N_GROUPS):
        if has_sink:
            srow = lax.broadcasted_iota(jnp.int32, (rows, LANES), 0)
            sink = jnp.zeros((rows, LANES), F32)
            for r in range(N_REP):
                sink = jnp.where(srow >= r * tile, sink_ref[g * N_REP + r], sink)
            m_old = m_scr[g]
            m_new = jnp.maximum(m_old, sink)
            acc = acc_scr[g] * jnp.exp(m_old - m_new) + jnp.where(half == g, 0.0, jnp.exp(sink - m_new))
        else:
            acc = acc_scr[g]
        outs.append(acc / pltpu.roll(acc, HEAD_DIM, axis=1))
    _store_slabs(o_ref, outs, tile)


def _attention(p16, q_blk, k_blk, v_blk, *, tile, mode, window=None, ns=None, sinks=None):
    b, t, _ = p16.shape
    tile = min(tile, t)
    n_prev = 0 if mode == "sel" else window // tile
    assert mode == "sel" or window % tile == 0
    has_sink = sinks is not None
    rows = N_REP * tile
    in_specs, args = [], []
    if has_sink:
        in_specs.append(pl.BlockSpec(memory_space=pltpu.SMEM))
        args.append(sinks.astype(F32))
    in_specs.append(pl.BlockSpec((1, tile, 512), lambda bi, i: (bi, i, q_blk)))
    args.append(p16)
    if mode == "sel":
        in_specs.append(pl.BlockSpec((1, tile, LANES), lambda bi, i: (bi, i, 0)))
        args.append(ns)
    in_specs.append(pl.BlockSpec((1, t, LANES), lambda bi, i: (bi, 0, k_blk)))
    in_specs.append(pl.BlockSpec((1, t, LANES), lambda bi, i: (bi, 0, v_blk)))
    args += [p16, p16]
    return pl.pallas_call(
        functools.partial(_attn_kernel, tile=tile, seq=t, mode=mode, n_prev=n_prev, has_sink=has_sink),
        grid=(b, t // tile),
        in_specs=in_specs,
        out_specs=pl.BlockSpec((1, tile, 512), lambda bi, i: (bi, i, 0)),
        out_shape=jax.ShapeDtypeStruct((b, t, 512), F32),
        scratch_shapes=[
            pltpu.VMEM((N_GROUPS, t, AUG_K), MXU_DTYPE),
            pltpu.VMEM((N_GROUPS, t, LANES), MXU_DTYPE),
            pltpu.VMEM((N_GROUPS, rows, AUG_K), MXU_DTYPE),
            pltpu.VMEM((N_GROUPS, rows, LANES), F32),
            pltpu.VMEM((N_GROUPS, rows, LANES), F32),
        ],
        compiler_params=_params("arbitrary", "arbitrary"),
        name="attn_" + mode + ("_sink" if has_sink else "") + str(tile),
    )(*args)


N_BRANCH = 3
GATE_EXP_W = N_BRANCH * N_REP * LANES


def _combine_kernel(h_ref, oc_ref, os_ref, ow_ref, ob_ref, gt_ref, e_ref, w_ref, o_ref):
    sig = jax.nn.sigmoid(gt_ref[...])
    hi = sig.astype(MXU_DTYPE)
    lo = (sig - hi.astype(F32)).astype(MXU_DTYPE)
    gx = _dot(hi, e_ref[...]) + _dot(lo, e_ref[...])
    acc = h_ref[...]
    branches = (oc_ref, os_ref, ow_ref)
    for r in range(N_REP):
        sl = slice(r * LANES, (r + 1) * LANES)
        o_a = jnp.zeros_like(oc_ref[:, sl])
        for br in range(N_BRANCH):
            c = (br * N_REP + r) * LANES
            o_a = o_a + gx[:, c:c + LANES] * branches[br][:, sl]
        acc = acc + _dot(o_a.astype(MXU_DTYPE), w_ref[sl, :])
    acc = acc + _dot(ob_ref[...].astype(MXU_DTYPE), w_ref[512:1024, :])
    o_ref[...] = acc


def _gate_expand():
    e = np.zeros((LANES, GATE_EXP_W), np.float32)
    for br in range(N_BRANCH):
        for r in range(N_REP):
            for g in range(N_GROUPS):
                c0 = (br * N_REP + r) * LANES + g * HEAD_DIM
                e[br * 8 + r * 2 + g, c0:c0 + HEAD_DIM] = 1.0
    return e


def _combine(h2, o_cmp, o_slc, o_win, o_b, gates, w_out_p, *, tm=512):
    m = h2.shape[0]
    tm = min(tm, m)
    row = lambda i: (i, 0)
    const = lambda i: (0, 0)
    e = jnp.asarray(_gate_expand(), MXU_DTYPE)
    return pl.pallas_call(
        _combine_kernel,
        grid=(m // tm,),
        in_specs=[
            pl.BlockSpec((tm, D_MODEL), row),
            pl.BlockSpec((tm, 512), row),
            pl.BlockSpec((tm, 512), row),
            pl.BlockSpec((tm, 512), row),
            pl.BlockSpec((tm, 512), row),
            pl.BlockSpec((tm, LANES), row),
            pl.BlockSpec((LANES, GATE_EXP_W), const),
            pl.BlockSpec((D_MODEL, D_MODEL), const),
        ],
        out_specs=pl.BlockSpec((tm, D_MODEL), row),
        out_shape=jax.ShapeDtypeStruct((m, D_MODEL), F32),
        compiler_params=_params("arbitrary"),
        name="combine_out",
    )(h2, o_cmp, o_slc, o_win, o_b, gates, e, w_out_p)


def _slab_cols(w):
    lead = w.shape[:-1]
    return w.reshape(*lead, N_GROUPS, N_REP, HEAD_DIM).swapaxes(-3, -2).reshape(*lead, N_GROUPS * N_REP * HEAD_DIM)


def _arrange_proj_weight(w):
    q_a = _slab_cols(w[:, 0:512])
    kc, vc, ks, vs, kw, vw = [w[:, 512 + j * LANES:512 + (j + 1) * LANES] for j in range(6)]
    gates = w[:, 1280:1304].reshape(-1, N_GROUPS, N_REP, N_BRANCH).transpose(0, 3, 2, 1).reshape(-1, 24)
    gates = jnp.pad(gates, ((0, 0), (0, LANES - 24)))
    q_b = _slab_cols(w[:, 1304:1816])
    k_b, v_b = w[:, 1816:1944], w[:, 1944:2072]
    return jnp.concatenate([q_a, q_b, ks, vs, kw, vw, k_b, v_b, kc, vc, gates], axis=1).astype(MXU_DTYPE)


def _arrange_out_weight(w):
    w_a = _slab_cols(w[0:512].T).T
    w_b = _slab_cols(w[512:1024].T).T
    return jnp.concatenate([w_a, w_b], axis=0).astype(MXU_DTYPE)


_QA_BLK, _QB_BLK = 0, 1
_KS_BLK, _VS_BLK, _KW_BLK, _VW_BLK, _KB_BLK, _VB_BLK = 8, 9, 10, 11, 12, 13


def kernel(x, ffn1_norm, ffn1_w_in, ffn1_w_out, mix_norm, w_mix_in, cmp_k_pos, cmp_k_w1, cmp_k_b1, cmp_k_w2, cmp_v_pos, cmp_v_w1, cmp_v_b1, cmp_v_w2, swa_sinks, w_mix_out, ffn2_norm, ffn2_w_in, ffn2_w_out, final_norm):
    b, t, d = x.shape
    m = b * t
    depth = ffn1_norm.shape[0]
    h = x.reshape(m, d)
    for l in range(depth):
        last = l == depth - 1
        h = _ffn(h, ffn1_norm[l], ffn1_w_in[l].astype(MXU_DTYPE), ffn1_w_out[l].astype(MXU_DTYPE))
        p16, kc_raw, vc_raw, gates = _proj(h, mix_norm[l], _arrange_proj_weight(w_mix_in[l]))
        p16 = p16.reshape(b, t, P16_W)
        kc = _compress(kc_raw.reshape(b, t, LANES), cmp_k_pos[l], cmp_k_w1[l], cmp_k_b1[l], cmp_k_w2[l])
        vc = _compress(vc_raw.reshape(b, t, LANES), cmp_v_pos[l], cmp_v_w1[l], cmp_v_b1[l], cmp_v_w2[l])
        o_cmp, notsel = _cmp_attention(p16, kc, vc)
        o_slc = _attention(p16, _QA_BLK, _KS_BLK, _VS_BLK, tile=256, mode="sel", ns=notsel)
        o_win = _attention(p16, _QA_BLK, _KW_BLK, _VW_BLK, tile=256, mode="band", window=NSA_WINDOW)
        o_b = _attention(p16, _QB_BLK, _KB_BLK, _VB_BLK, tile=128, mode="band", window=SWA_WINDOW, sinks=swa_sinks[l])
        h = _combine(h, o_cmp.reshape(m, 512), o_slc.reshape(m, 512), o_win.reshape(m, 512), o_b.reshape(m, 512),
                     gates, _arrange_out_weight(w_mix_out[l]))
        h = _ffn(h, ffn2_norm[l], ffn2_w_in[l].astype(MXU_DTYPE), ffn2_w_out[l].astype(MXU_DTYPE),
                 final_norm if last else None)
    if depth == 0:
        raise ValueError("depth must be positive")
    return h.reshape(b, t, d)
```

```python
import functools

import numpy as np
import jax
import jax.numpy as jnp
from jax import lax
from jax.experimental import pallas as pl
from jax.experimental.pallas import tpu as pltpu

F32 = jnp.float32
MXU_DTYPE = jnp.bfloat16

D_MODEL = 1024
HEAD_DIM = 64
N_GROUPS = 2
N_REP = 4
CMP_BLOCK = 32
CMP_STRIDE = 16
CMP_HIDDEN = 256
SEL_BLOCK = 64
SEL_TOPK = 16
NSA_WINDOW = 512
SWA_WINDOW = 128
D_FF = 2816
NORM_EPS = 1e-6
NEG_INF = -1e30
FORCE_SCORE = 1e9
ATTN_SCALE = HEAD_DIM ** -0.5
MASK_BIG = 2.0 ** 100
LANES = 128
AUG_K = 2 * LANES
_BLK_SHIFT = 6
_BLK_MASK = SEL_BLOCK - 1
VMEM_LIMIT = 48 * 1024 * 1024

_SLOPES = (2.0 ** (-8.0 * np.arange(1, 9) / 8)).reshape(N_GROUPS, N_REP)

_NT = (((1,), (1,)), ((), ()))


def _dot(a, b):
    return jnp.dot(a, b, preferred_element_type=F32)


def _dot_nt(a, b):
    return lax.dot_general(a, b, _NT, preferred_element_type=F32)


def _rmsnorm(x, g):
    ms = jnp.mean(x * x, axis=-1, keepdims=True)
    return x * lax.rsqrt(ms + NORM_EPS) * g


def _params(*sem):
    return pltpu.CompilerParams(dimension_semantics=sem, vmem_limit_bytes=VMEM_LIMIT)


def _ffn_kernel(x_ref, g_ref, wg_ref, wu_ref, wo_ref, *rest, n_f, final_norm):
    if final_norm:
        fg_ref, o_ref, xn_scr, acc_scr = rest
    else:
        o_ref, xn_scr, acc_scr = rest
    f = pl.program_id(1)

    @pl.when(f == 0)
    def _():
        xn_scr[...] = _rmsnorm(x_ref[...], g_ref[...]).astype(MXU_DTYPE)
        acc_scr[...] = jnp.zeros_like(acc_scr)

    xn = xn_scr[...]
    gate = _dot(xn, wg_ref[...])
    up = _dot(xn, wu_ref[...])
    act = gate * jax.nn.sigmoid(gate) * up
    acc_scr[...] += _dot(act.astype(MXU_DTYPE), wo_ref[...])

    @pl.when(f == n_f - 1)
    def _():
        h = x_ref[...] + 0.5 * acc_scr[...]
        if final_norm:
            h = _rmsnorm(h, fg_ref[...])
        o_ref[...] = h


def _ffn(x2, norm_g, w_in, w_out, final_g=None, *, tm=1024, tf=256):
    m = x2.shape[0]
    tm = min(tm, m)
    n_f = D_FF // tf
    final_norm = final_g is not None
    in_specs = [
        pl.BlockSpec((tm, D_MODEL), lambda i, f: (i, 0)),
        pl.BlockSpec((1, D_MODEL), lambda i, f: (0, 0)),
        pl.BlockSpec((D_MODEL, tf), lambda i, f: (0, f)),
        pl.BlockSpec((D_MODEL, tf), lambda i, f: (0, n_f + f)),
        pl.BlockSpec((tf, D_MODEL), lambda i, f: (f, 0)),
    ]
    args = [x2, norm_g.reshape(1, D_MODEL), w_in, w_in, w_out]
    if final_norm:
        in_specs.append(pl.BlockSpec((1, D_MODEL), lambda i, f: (0, 0)))
        args.append(final_g.reshape(1, D_MODEL))
    return pl.pallas_call(
        functools.partial(_ffn_kernel, n_f=n_f, final_norm=final_norm),
        grid=(m // tm, n_f),
        in_specs=in_specs,
        out_specs=pl.BlockSpec((tm, D_MODEL), lambda i, f: (i, 0)),
        out_shape=jax.ShapeDtypeStruct((m, D_MODEL), F32),
        scratch_shapes=[pltpu.VMEM((tm, D_MODEL), MXU_DTYPE), pltpu.VMEM((tm, D_MODEL), F32)],
        compiler_params=_params("arbitrary", "arbitrary"),
        name="ffn_final" if final_norm else "ffn",
    )(*args)


P16_W = 1024 + 6 * LANES
PROJ_W = P16_W + 3 * LANES


def _proj_kernel(h_ref, g_ref, w_ref, p16_ref, kc_ref, vc_ref, gt_ref):
    xn = _rmsnorm(h_ref[...], g_ref[...]).astype(MXU_DTYPE)
    y = _dot(xn, w_ref[...])
    p16_ref[:, 0:1024] = (y[:, 0:1024] * ATTN_SCALE).astype(MXU_DTYPE)
    p16_ref[:, 1024:P16_W] = y[:, 1024:P16_W].astype(MXU_DTYPE)
    kc_ref[...] = y[:, P16_W:P16_W + LANES]
    vc_ref[...] = y[:, P16_W + LANES:P16_W + 2 * LANES]
    gt_ref[...] = y[:, P16_W + 2 * LANES:PROJ_W]


def _proj(h2, norm_g, w, *, tm=512):
    m = h2.shape[0]
    tm = min(tm, m)
    row = lambda i: (i, 0)
    return pl.pallas_call(
        _proj_kernel,
        grid=(m // tm,),
        in_specs=[
            pl.BlockSpec((tm, D_MODEL), row),
            pl.BlockSpec((1, D_MODEL), lambda i: (0, 0)),
            pl.BlockSpec((D_MODEL, PROJ_W), lambda i: (0, 0)),
        ],
        out_specs=[
            pl.BlockSpec((tm, P16_W), row),
            pl.BlockSpec((tm, LANES), row),
            pl.BlockSpec((tm, LANES), row),
            pl.BlockSpec((tm, LANES), row),
        ],
        out_shape=[
            jax.ShapeDtypeStruct((m, P16_W), MXU_DTYPE),
            jax.ShapeDtypeStruct((m, LANES), F32),
            jax.ShapeDtypeStruct((m, LANES), F32),
            jax.ShapeDtypeStruct((m, LANES), F32),
        ],
        compiler_params=_params("arbitrary"),
        name="mix_proj",
    )(h2, norm_g.reshape(1, D_MODEL), w)


def _gelu_tanh(x):
    return 0.5 * x * (1.0 + jnp.tanh(np.sqrt(2.0 / np.pi) * (x + 0.044715 * (x * x * x))))


def _compress_kernel(raw_ref, pa_ref, pb_ref, wa_ref, wb_ref, b1_ref, w2_ref, o_ref):
    x = raw_ref[0]
    n_rows = x.shape[0]
    first = _dot((x + pa_ref[...]).astype(MXU_DTYPE), wa_ref[...])
    second = _dot((x + pb_ref[...]).astype(MXU_DTYPE), wb_ref[...])
    hid = first + pltpu.roll(second, n_rows - 1, axis=0) + b1_ref[...]
    out = _dot(_gelu_tanh(hid).astype(MXU_DTYPE), w2_ref[...])
    row = lax.broadcasted_iota(jnp.int32, out.shape, 0)
    o_ref[0] = jnp.where(row == n_rows - 1, 0.0, out).astype(o_ref.dtype)


def _compress(raw, pos, w1, b1, w2):
    b, t, _ = raw.shape
    n_rows = t // CMP_STRIDE
    half = CMP_STRIDE
    kw = half * LANES
    eye = jnp.eye(N_GROUPS, dtype=F32)
    w1r = w1.reshape(CMP_BLOCK, HEAD_DIM, CMP_HIDDEN)

    def expand(wpart):
        return jnp.einsum("jdh,ab->jadbh", wpart, eye).reshape(kw, N_GROUPS * CMP_HIDDEN).astype(MXU_DTYPE)

    def expand_pos(ppart):
        return jnp.broadcast_to(ppart[:, None, :], (half, N_GROUPS, HEAD_DIM)).reshape(1, kw)

    wa, wb = expand(w1r[:half]), expand(w1r[half:])
    pa, pb = expand_pos(pos[:half]), expand_pos(pos[half:])
    b1e = jnp.tile(b1.reshape(1, CMP_HIDDEN), (1, N_GROUPS))
    w2e = jnp.einsum("hd,ab->ahbd", w2, eye).reshape(N_GROUPS * CMP_HIDDEN, LANES).astype(MXU_DTYPE)
    const = lambda i: (0, 0)
    return pl.pallas_call(
        _compress_kernel,
        grid=(b,),
        in_specs=[
            pl.BlockSpec((1, n_rows, kw), lambda i: (i, 0, 0)),
            pl.BlockSpec((1, kw), const),
            pl.BlockSpec((1, kw), const),
            pl.BlockSpec((kw, N_GROUPS * CMP_HIDDEN), const),
            pl.BlockSpec((kw, N_GROUPS * CMP_HIDDEN), const),
            pl.BlockSpec((1, N_GROUPS * CMP_HIDDEN), const),
            pl.BlockSpec((N_GROUPS * CMP_HIDDEN, LANES), const),
        ],
        out_specs=pl.BlockSpec((1, n_rows, LANES), lambda i: (i, 0, 0)),
        out_shape=jax.ShapeDtypeStruct((b, n_rows, LANES), MXU_DTYPE),
        compiler_params=_params("arbitrary"),
        name="compress",
    )(raw.reshape(b, n_rows, kw), pa, pb, wa, wb, b1e, w2e)


def _lane_half(shape):
    return lax.broadcasted_iota(jnp.int32, shape, len(shape) - 1) >> _BLK_SHIFT


def _alibi_cols(c0, c1, c2, c3, shape):
    lane = lax.broadcasted_iota(jnp.int32, shape, 1)
    z = jnp.zeros(shape, F32)
    return jnp.where(lane == 0, c0, jnp.where(lane == 1, c1, jnp.where(lane == 2, c2, jnp.where(lane == 3, c3, z))))


def _q_alibi(tq, t0, slope):
    t = t0 + lax.broadcasted_iota(jnp.int32, (tq, LANES), 0)
    hi = (t >> _BLK_SHIFT).astype(F32)
    lo = (t & _BLK_MASK).astype(F32)
    return _alibi_cols(slope * SEL_BLOCK * hi, slope * lo, slope * SEL_BLOCK, slope, (tq, LANES)).astype(MXU_DTYPE)


def _k_alibi(pos):
    hi = (pos >> _BLK_SHIFT).astype(F32)
    lo = (pos & _BLK_MASK).astype(F32)
    return _alibi_cols(-1.0, -1.0, hi, lo, pos.shape).astype(MXU_DTYPE)


def _build_qaug(qaug_scr, q_ref, t0, tq, other):
    half = _lane_half((tq, LANES))
    for r in range(N_REP):
        slab = q_ref[0, :, r * LANES:(r + 1) * LANES]
        for g in range(N_GROUPS):
            fill = jnp.zeros_like(slab) if other is None else other
            qaug_scr[g, r * tq:(r + 1) * tq, 0:LANES] = jnp.where(half == g, slab, fill)
            qaug_scr[g, r * tq:(r + 1) * tq, LANES:AUG_K] = _q_alibi(tq, t0, float(_SLOPES[g, r]))


def _store_slabs(o_ref, per_group, tq):
    lane = lax.broadcasted_iota(jnp.int32, (tq, LANES), 1)
    for r in range(N_REP):
        a = per_group[0][r * tq:(r + 1) * tq]
        b = per_group[1][r * tq:(r + 1) * tq]
        o_ref[0, :, r * LANES:(r + 1) * LANES] = jnp.where(lane < HEAD_DIM, a, b).astype(o_ref.dtype)


def _cmp_kernel(q_ref, kc_ref, vc_ref, ovl_ref, o_ref, ns_ref, qaug_scr, *, tq, n_cmp, n_sel):
    i = pl.program_id(1)
    t0 = i * tq
    _build_qaug(qaug_scr, q_ref, t0, tq, None)

    kc = kc_ref[0]
    vc = vc_ref[0]
    half_k = _lane_half((n_cmp, LANES))
    c_pos = lax.broadcasted_iota(jnp.int32, (n_cmp, LANES), 0) * CMP_STRIDE + (CMP_BLOCK - 1)
    k_ali = _k_alibi(c_pos)
    t_row = t0 + lax.broadcasted_iota(jnp.int32, (tq, n_cmp), 0)
    c_end = lax.broadcasted_iota(jnp.int32, (tq, n_cmp), 1) * CMP_STRIDE + (CMP_BLOCK - 1)
    vis = (c_end <= t_row)[None]

    outs, psl_t = [], []
    for g in range(N_GROUPS):
        kaug = jnp.concatenate([jnp.where(half_k == g, kc, jnp.zeros_like(kc)), k_ali], axis=1)
        s = _dot_nt(qaug_scr[g], kaug).reshape(N_REP, tq, n_cmp)
        s = jnp.where(vis, s, NEG_INF)
        m = jnp.max(s, axis=-1, keepdims=True)
        e = jnp.where(vis, jnp.exp(s - m), 0.0)
        l = jnp.sum(e, axis=-1, keepdims=True)
        p = e * jnp.where(l > 0.0, 1.0 / l, 0.0)
        pb = p.astype(MXU_DTYPE).reshape(N_REP * tq, n_cmp)
        outs.append(_dot(pb, vc))
        pt = _dot_nt(ovl_ref[...], pb)
        psl_t.append(pt[:, 0:tq] + pt[:, tq:2 * tq] + pt[:, 2 * tq:3 * tq] + pt[:, 3 * tq:4 * tq])
    _store_slabs(o_ref, outs, tq)

    blk = lax.broadcasted_iota(jnp.int32, (SEL_BLOCK, tq), 0)
    cur = (t0 + lax.broadcasted_iota(jnp.int32, (SEL_BLOCK, tq), 1)) >> _BLK_SHIFT
    valid = blk <= cur
    forced = (blk == 0) | (blk == cur) | (blk == cur - 1)
    notsel = []
    for g in (1, 0):
        score = jnp.where(forced, FORCE_SCORE, jnp.where(valid, psl_t[g][0:SEL_BLOCK], NEG_INF))
        cnt = jnp.zeros((SEL_BLOCK, tq), jnp.int32)
        for j in range(n_sel):
            xj = jnp.broadcast_to(score[j:j + 1, :], (SEL_BLOCK, tq))
            beats = (xj > score) | ((xj == score) & (blk > j))
            cnt = cnt + jnp.where(beats, 1, 0)
        keep = (cnt < SEL_TOPK) & valid
        notsel.append(jnp.where(keep, 0.0, 1.0))
    ns_ref[0] = jnp.transpose(jnp.concatenate(notsel, axis=0)).astype(ns_ref.dtype)


def _overlap_t(t):
    nc_pad = t // CMP_STRIDE
    ns = t // SEL_BLOCK
    c_start = np.arange(nc_pad) * CMP_STRIDE
    s_start = np.arange(ns) * SEL_BLOCK
    ov = np.clip(np.minimum(c_start[:, None] + CMP_BLOCK, s_start[None, :] + SEL_BLOCK)
                 - np.maximum(c_start[:, None], s_start[None, :]), 0, None) / CMP_BLOCK
    ov[nc_pad - 1] = 0.0
    out = np.zeros((LANES, nc_pad), np.float32)
    out[0:ns] = ov.T
    out[HEAD_DIM:HEAD_DIM + ns] = ov.T
    return out


def _cmp_attention(p16, kc, vc, *, tq=256):
    b, t, _ = p16.shape
    tq = min(tq, t)
    n_cmp = t // CMP_STRIDE
    n_sel = t // SEL_BLOCK
    assert n_sel <= SEL_BLOCK and n_cmp % LANES == 0
    ovl = jnp.asarray(_overlap_t(t), MXU_DTYPE)
    return pl.pallas_call(
        functools.partial(_cmp_kernel, tq=tq, n_cmp=n_cmp, n_sel=n_sel),
        grid=(b, t // tq),
        in_specs=[
            pl.BlockSpec((1, tq, 512), lambda bi, i: (bi, i, 0)),
            pl.BlockSpec((1, n_cmp, LANES), lambda bi, i: (bi, 0, 0)),
            pl.BlockSpec((1, n_cmp, LANES), lambda bi, i: (bi, 0, 0)),
            pl.BlockSpec((LANES, n_cmp), lambda bi, i: (0, 0)),
        ],
        out_specs=[
            pl.BlockSpec((1, tq, 512), lambda bi, i: (bi, i, 0)),
            pl.BlockSpec((1, tq, LANES), lambda bi, i: (bi, i, 0)),
        ],
        out_shape=[
            jax.ShapeDtypeStruct((b, t, 512), F32),
            jax.ShapeDtypeStruct((b, t, LANES), MXU_DTYPE),
        ],
        scratch_shapes=[pltpu.VMEM((N_GROUPS, N_REP * tq, AUG_K), MXU_DTYPE)],
        compiler_params=_params("arbitrary", "arbitrary"),
        name="cmp_attn",
    )(p16, kc, vc, ovl)


def _attn_kernel(*refs, tile, seq, mode, n_prev, span, window, has_sink):
    refs = list(refs)
    sink_ref = refs.pop(0) if has_sink else None
    q_ref = refs.pop(0)
    ns_ref = refs.pop(0) if mode == "sel" else None
    k_ref, v_ref, o_ref, kaug_scr, vaug_scr, qaug_scr, acc_scr, m_scr = refs
    i = pl.program_id(1)
    t0 = i * tile
    rows = N_REP * tile

    @pl.when(i == 0)
    def _():
        def body(c, carry):
            r0 = pl.multiple_of(c * tile, tile)
            k = k_ref[0, pl.ds(r0, tile), :]
            v = v_ref[0, pl.ds(r0, tile), :].astype(F32)
            half = _lane_half((tile, LANES))
            pos = r0 + lax.broadcasted_iota(jnp.int32, (tile, LANES), 0)
            k_ali = _k_alibi(pos)
            lane = lax.broadcasted_iota(jnp.int32, (tile, LANES), 1)
            if mode == "sel":
                off = jnp.where((lane & _BLK_MASK) == (pos >> _BLK_SHIFT), -MASK_BIG, 0.0).astype(MXU_DTYPE)
            else:
                off = jnp.zeros((tile, LANES), MXU_DTYPE)
            for g in range(N_GROUPS):
                kaug_scr[g, pl.ds(r0, tile), 0:LANES] = jnp.where(half == g, k, off)
                kaug_scr[g, pl.ds(r0, tile), LANES:AUG_K] = k_ali
                vaug_scr[g, c] = jnp.transpose(jnp.where(half == g, v, 1.0)).astype(MXU_DTYPE)
            return carry
        lax.fori_loop(0, seq // tile, body, 0)

    _build_qaug(qaug_scr, q_ref, t0, tile, ns_ref[0] if mode == "sel" else None)
    acc_scr[...] = jnp.zeros_like(acc_scr)
    m_scr[...] = jnp.full_like(m_scr, -3e38)

    def attend(j0, off, masked):
        kw = span * tile
        keep = None
        if masked:
            key = lax.broadcasted_iota(jnp.int32, (kw, rows), 0)
            qry = lax.broadcasted_iota(jnp.int32, (kw, rows), 1) & (tile - 1)
            dist = qry - key + off
            keep = dist >= 0
            if window is not None:
                keep = keep & (dist < window)
        k0 = pl.multiple_of(j0 * tile, tile)
        scores = [_dot_nt(kaug_scr[g, pl.ds(k0, kw), :], qaug_scr[g]) for g in range(N_GROUPS)]
        for g in range(N_GROUPS):
            s = scores[g]
            if keep is not None:
                s = jnp.where(keep, s, NEG_INF)
            m_old = m_scr[g]
            m_new = jnp.maximum(m_old, jnp.max(s, axis=0, keepdims=True))
            p = jnp.exp(s - m_new).astype(MXU_DTYPE)
            pv = _dot(vaug_scr[g, j0], p[0:tile])
            for n in range(1, span):
                pv = pv + _dot(vaug_scr[g, j0 + n], p[n * tile:(n + 1) * tile])
            acc_scr[g] = acc_scr[g] * jnp.exp(m_old - m_new) + pv
            m_scr[g] = m_new

    if mode == "sel":
        n_full = i // span

        def body(jj, carry):
            attend(jj * span, 0, False)
            return carry
        lax.fori_loop(0, n_full, body, 0)
        attend(n_full * span, (i - n_full * span) * tile, True)
    else:
        j0 = jnp.maximum(i - n_prev, 0)
        attend(j0, (i - j0) * tile, True)

    outs = []
    sub = lax.broadcasted_iota(jnp.int32, (LANES, rows), 0)
    for g in range(N_GROUPS):
        acc = acc_scr[g]
        if has_sink:
            col = lax.broadcasted_iota(jnp.int32, (1, rows), 1)
            sink = jnp.zeros((1, rows), F32)
            for r in range(N_REP):
                sink = jnp.where(col >= r * tile, sink_ref[g * N_REP + r], sink)
            m_old = m_scr[g]
            m_new = jnp.maximum(m_old, sink)
            in_group = (sub >> _BLK_SHIFT) == g
            acc = acc * jnp.exp(m_old - m_new) + jnp.where(in_group, 0.0, jnp.exp(sink - m_new))
        den_row = (1 - g) * HEAD_DIM
        outs.append(acc / acc[den_row:den_row + 1, :])
    both = jnp.where(sub < HEAD_DIM, outs[0], outs[1])
    for r in range(N_REP):
        o_ref[0, :, r * LANES:(r + 1) * LANES] = jnp.transpose(both[:, r * tile:(r + 1) * tile]).astype(o_ref.dtype)


def _attention(p16, q_blk, k_blk, v_blk, *, tile, mode, span=1, window=None, ns=None, sinks=None):
    b, t, _ = p16.shape
    tile = min(tile, t)
    n_prev = 0 if mode == "sel" else -(-(window - 1) // tile)
    if mode == "band":
        span = n_prev + 1
    assert (t // tile) % span == 0 or mode == "band"
    assert span * tile <= t
    has_sink = sinks is not None
    rows = N_REP * tile
    in_specs, args = [], []
    if has_sink:
        in_specs.append(pl.BlockSpec(memory_space=pltpu.SMEM))
        args.append(sinks.astype(F32))
    in_specs.append(pl.BlockSpec((1, tile, 512), lambda bi, i: (bi, i, q_blk)))
    args.append(p16)
    if mode == "sel":
        in_specs.append(pl.BlockSpec((1, tile, LANES), lambda bi, i: (bi, i, 0)))
        args.append(ns)
    in_specs.append(pl.BlockSpec((1, t, LANES), lambda bi, i: (bi, 0, k_blk)))
    in_specs.append(pl.BlockSpec((1, t, LANES), lambda bi, i: (bi, 0, v_blk)))
    args += [p16, p16]
    return pl.pallas_call(
        functools.partial(_attn_kernel, tile=tile, seq=t, mode=mode, n_prev=n_prev, span=span, window=window, has_sink=has_sink),
        grid=(b, t // tile),
        in_specs=in_specs,
        out_specs=pl.BlockSpec((1, tile, 512), lambda bi, i: (bi, i, 0)),
        out_shape=jax.ShapeDtypeStruct((b, t, 512), F32),
        scratch_shapes=[
            pltpu.VMEM((N_GROUPS, t, AUG_K), MXU_DTYPE),
            pltpu.VMEM((N_GROUPS, t // tile, LANES, tile), MXU_DTYPE),
            pltpu.VMEM((N_GROUPS, rows, AUG_K), MXU_DTYPE),
            pltpu.VMEM((N_GROUPS, LANES, rows), F32),
            pltpu.VMEM((N_GROUPS, 1, rows), F32),
        ],
        compiler_params=_params("arbitrary", "arbitrary"),
        name="attn_" + mode + ("_sink" if has_sink else "") + str(tile),
    )(*args)


N_BRANCH = 3
GATE_EXP_W = N_BRANCH * N_REP * LANES


def _combine_kernel(h_ref, oc_ref, os_ref, ow_ref, ob_ref, gt_ref, e_ref, w_ref, o_ref):
    sig = jax.nn.sigmoid(gt_ref[...])
    hi = sig.astype(MXU_DTYPE)
    lo = (sig - hi.astype(F32)).astype(MXU_DTYPE)
    gx = _dot(hi, e_ref[...]) + _dot(lo, e_ref[...])
    acc = h_ref[...]
    branches = (oc_ref, os_ref, ow_ref)
    for r in range(N_REP):
        sl = slice(r * LANES, (r + 1) * LANES)
        o_a = jnp.zeros_like(oc_ref[:, sl])
        for br in range(N_BRANCH):
            c = (br * N_REP + r) * LANES
            o_a = o_a + gx[:, c:c + LANES] * branches[br][:, sl]
        acc = acc + _dot(o_a.astype(MXU_DTYPE), w_ref[sl, :])
    acc = acc + _dot(ob_ref[...].astype(MXU_DTYPE), w_ref[512:1024, :])
    o_ref[...] = acc


def _gate_expand():
    e = np.zeros((LANES, GATE_EXP_W), np.float32)
    for br in range(N_BRANCH):
        for r in range(N_REP):
            for g in range(N_GROUPS):
                c0 = (br * N_REP + r) * LANES + g * HEAD_DIM
                e[br * 8 + r * 2 + g, c0:c0 + HEAD_DIM] = 1.0
    return e


def _combine(h2, o_cmp, o_slc, o_win, o_b, gates, w_out_p, *, tm=512):
    m = h2.shape[0]
    tm = min(tm, m)
    row = lambda i: (i, 0)
    const = lambda i: (0, 0)
    e = jnp.asarray(_gate_expand(), MXU_DTYPE)
    return pl.pallas_call(
        _combine_kernel,
        grid=(m // tm,),
        in_specs=[
            pl.BlockSpec((tm, D_MODEL), row),
            pl.BlockSpec((tm, 512), row),
            pl.BlockSpec((tm, 512), row),
            pl.BlockSpec((tm, 512), row),
            pl.BlockSpec((tm, 512), row),
            pl.BlockSpec((tm, LANES), row),
            pl.BlockSpec((LANES, GATE_EXP_W), const),
            pl.BlockSpec((D_MODEL, D_MODEL), const),
        ],
        out_specs=pl.BlockSpec((tm, D_MODEL), row),
        out_shape=jax.ShapeDtypeStruct((m, D_MODEL), F32),
        compiler_params=_params("arbitrary"),
        name="combine_out",
    )(h2, o_cmp, o_slc, o_win, o_b, gates, e, w_out_p)


def _slab_cols(w):
    lead = w.shape[:-1]
    return w.reshape(*lead, N_GROUPS, N_REP, HEAD_DIM).swapaxes(-3, -2).reshape(*lead, N_GROUPS * N_REP * HEAD_DIM)


def _arrange_proj_weight(w):
    q_a = _slab_cols(w[:, 0:512])
    kc, vc, ks, vs, kw, vw = [w[:, 512 + j * LANES:512 + (j + 1) * LANES] for j in range(6)]
    gates = w[:, 1280:1304].reshape(-1, N_GROUPS, N_REP, N_BRANCH).transpose(0, 3, 2, 1).reshape(-1, 24)
    gates = jnp.pad(gates, ((0, 0), (0, LANES - 24)))
    q_b = _slab_cols(w[:, 1304:1816])
    k_b, v_b = w[:, 1816:1944], w[:, 1944:2072]
    return jnp.concatenate([q_a, q_b, ks, vs, kw, vw, k_b, v_b, kc, vc, gates], axis=1).astype(MXU_DTYPE)


def _arrange_out_weight(w):
    w_a = _slab_cols(w[0:512].T).T
    w_b = _slab_cols(w[512:1024].T).T
    return jnp.concatenate([w_a, w_b], axis=0).astype(MXU_DTYPE)


_QA_BLK, _QB_BLK = 0, 1
_KS_BLK, _VS_BLK, _KW_BLK, _VW_BLK, _KB_BLK, _VB_BLK = 8, 9, 10, 11, 12, 13


def kernel(x, ffn1_norm, ffn1_w_in, ffn1_w_out, mix_norm, w_mix_in, cmp_k_pos, cmp_k_w1, cmp_k_b1, cmp_k_w2, cmp_v_pos, cmp_v_w1, cmp_v_b1, cmp_v_w2, swa_sinks, w_mix_out, ffn2_norm, ffn2_w_in, ffn2_w_out, final_norm):
    b, t, d = x.shape
    m = b * t
    depth = ffn1_norm.shape[0]
    h = x.reshape(m, d)
    for l in range(depth):
        last = l == depth - 1
        h = _ffn(h, ffn1_norm[l], ffn1_w_in[l].astype(MXU_DTYPE), ffn1_w_out[l].astype(MXU_DTYPE))
        p16, kc_raw, vc_raw, gates = _proj(h, mix_norm[l], _arrange_proj_weight(w_mix_in[l]))
        p16 = p16.reshape(b, t, P16_W)
        kc = _compress(kc_raw.reshape(b, t, LANES), cmp_k_pos[l], cmp_k_w1[l], cmp_k_b1[l], cmp_k_w2[l])
        vc = _compress(vc_raw.reshape(b, t, LANES), cmp_v_pos[l], cmp_v_w1[l], cmp_v_b1[l], cmp_v_w2[l])
        o_cmp, notsel = _cmp_attention(p16, kc, vc)
        o_slc = _attention(p16, _QA_BLK, _KS_BLK, _VS_BLK, tile=256, mode="sel", span=2, ns=notsel)
        o_win = _attention(p16, _QA_BLK, _KW_BLK, _VW_BLK, tile=256, mode="band", window=NSA_WINDOW)
        o_b = _attention(p16, _QB_BLK, _KB_BLK, _VB_BLK, tile=256, mode="band", window=SWA_WINDOW, sinks=swa_sinks[l])
        h = _combine(h, o_cmp.reshape(m, 512), o_slc.reshape(m, 512), o_win.reshape(m, 512), o_b.reshape(m, 512),
                     gates, _arrange_out_weight(w_mix_out[l]))
        h = _ffn(h, ffn2_norm[l], ffn2_w_in[l].astype(MXU_DTYPE), ffn2_w_out[l].astype(MXU_DTYPE),
                 final_norm if last else None)
    if depth == 0:
        raise ValueError("depth must be positive")
    return h.reshape(b, t, d)
```

```python
import functools

import numpy as np
import jax
import jax.numpy as jnp
from jax import lax
from jax.experimental import pallas as pl
from jax.experimental.pallas import tpu as pltpu

F32 = jnp.float32
MXU_DTYPE = jnp.bfloat16

D_MODEL = 1024
HEAD_DIM = 64
N_GROUPS = 2
N_REP = 4
CMP_BLOCK = 32
CMP_STRIDE = 16
CMP_HIDDEN = 256
SEL_BLOCK = 64
SEL_TOPK = 16
NSA_WINDOW = 512
SWA_WINDOW = 128
D_FF = 2816
NORM_EPS = 1e-6
NEG_INF = -1e30
FORCE_SCORE = 1e9
ATTN_SCALE = HEAD_DIM ** -0.5
MASK_BIG = 2.0 ** 100
LANES = 128
AUG_K = 2 * LANES
_BLK_SHIFT = 6
_BLK_MASK = SEL_BLOCK - 1
VMEM_LIMIT = 48 * 1024 * 1024

_SLOPES = (2.0 ** (-8.0 * np.arange(1, 9) / 8)).reshape(N_GROUPS, N_REP)

_NT = (((1,), (1,)), ((), ()))


def _dot(a, b):
    return jnp.dot(a, b, preferred_element_type=F32)


def _dot_nt(a, b):
    return lax.dot_general(a, b, _NT, preferred_element_type=F32)


def _rmsnorm(x, g):
    ms = jnp.mean(x * x, axis=-1, keepdims=True)
    return x * lax.rsqrt(ms + NORM_EPS) * g


def _params(*sem):
    return pltpu.CompilerParams(dimension_semantics=sem, vmem_limit_bytes=VMEM_LIMIT)


def _ffn_kernel(x_ref, g_ref, wi_ref, wo_ref, *rest, n_f, final_norm):
    if final_norm:
        fg_ref, o_ref = rest
    else:
        (o_ref,) = rest
    x = x_ref[...]
    xn = _rmsnorm(x, g_ref[...]).astype(MXU_DTYPE)
    acc = None
    for f in range(n_f):
        gate = _dot(xn, wi_ref[f])
        up = _dot(xn, wi_ref[n_f + f])
        act = gate * jax.nn.sigmoid(gate) * up
        part = _dot(act.astype(MXU_DTYPE), wo_ref[f])
        acc = part if acc is None else acc + part
    h = x + 0.5 * acc
    if final_norm:
        h = _rmsnorm(h, fg_ref[...])
    o_ref[...] = h


def _ffn(x2, norm_g, w_in, w_out, final_g=None, *, tm=512, tf=256):
    m = x2.shape[0]
    tm = min(tm, m)
    n_f = D_FF // tf
    final_norm = final_g is not None
    wi = w_in.reshape(D_MODEL, 2 * n_f, tf).transpose(1, 0, 2)
    wo = w_out.reshape(n_f, tf, D_MODEL)
    resident = pl.Buffered(1)
    in_specs = [
        pl.BlockSpec((tm, D_MODEL), lambda i: (i, 0)),
        pl.BlockSpec((1, D_MODEL), lambda i: (0, 0)),
        pl.BlockSpec((2 * n_f, D_MODEL, tf), lambda i: (0, 0, 0), pipeline_mode=resident),
        pl.BlockSpec((n_f, tf, D_MODEL), lambda i: (0, 0, 0), pipeline_mode=resident),
    ]
    args = [x2, norm_g.reshape(1, D_MODEL), wi, wo]
    if final_norm:
        in_specs.append(pl.BlockSpec((1, D_MODEL), lambda i: (0, 0)))
        args.append(final_g.reshape(1, D_MODEL))
    return pl.pallas_call(
        functools.partial(_ffn_kernel, n_f=n_f, final_norm=final_norm),
        grid=(m // tm,),
        in_specs=in_specs,
        out_specs=pl.BlockSpec((tm, D_MODEL), lambda i: (i, 0)),
        out_shape=jax.ShapeDtypeStruct((m, D_MODEL), F32),
        compiler_params=_params("arbitrary"),
        name="ffn_final" if final_norm else "ffn",
    )(*args)


P16_W = 1024 + 6 * LANES
PROJ_W = P16_W + 3 * LANES


def _proj_kernel(h_ref, g_ref, w_ref, p16_ref, kc_ref, vc_ref, gt_ref):
    xn = _rmsnorm(h_ref[...], g_ref[...]).astype(MXU_DTYPE)
    y = _dot(xn, w_ref[...])
    p16_ref[:, 0:1024] = (y[:, 0:1024] * ATTN_SCALE).astype(MXU_DTYPE)
    p16_ref[:, 1024:P16_W] = y[:, 1024:P16_W].astype(MXU_DTYPE)
    kc_ref[...] = y[:, P16_W:P16_W + LANES]
    vc_ref[...] = y[:, P16_W + LANES:P16_W + 2 * LANES]
    gt_ref[...] = y[:, P16_W + 2 * LANES:PROJ_W]


def _proj(h2, norm_g, w, *, tm=512):
    m = h2.shape[0]
    tm = min(tm, m)
    row = lambda i: (i, 0)
    return pl.pallas_call(
        _proj_kernel,
        grid=(m // tm,),
        in_specs=[
            pl.BlockSpec((tm, D_MODEL), row),
            pl.BlockSpec((1, D_MODEL), lambda i: (0, 0)),
            pl.BlockSpec((D_MODEL, PROJ_W), lambda i: (0, 0)),
        ],
        out_specs=[
            pl.BlockSpec((tm, P16_W), row),
            pl.BlockSpec((tm, LANES), row),
            pl.BlockSpec((tm, LANES), row),
            pl.BlockSpec((tm, LANES), row),
        ],
        out_shape=[
            jax.ShapeDtypeStruct((m, P16_W), MXU_DTYPE),
            jax.ShapeDtypeStruct((m, LANES), F32),
            jax.ShapeDtypeStruct((m, LANES), F32),
            jax.ShapeDtypeStruct((m, LANES), F32),
        ],
        compiler_params=_params("arbitrary"),
        name="mix_proj",
    )(h2, norm_g.reshape(1, D_MODEL), w)


def _gelu_tanh(x):
    return 0.5 * x * (1.0 + jnp.tanh(np.sqrt(2.0 / np.pi) * (x + 0.044715 * (x * x * x))))


def _compress_kernel(raw_ref, pa_ref, pb_ref, wa_ref, wb_ref, b1_ref, w2_ref, o_ref):
    x = raw_ref[0]
    n_rows = x.shape[0]
    first = _dot((x + pa_ref[...]).astype(MXU_DTYPE), wa_ref[...])
    second = _dot((x + pb_ref[...]).astype(MXU_DTYPE), wb_ref[...])
    hid = first + pltpu.roll(second, n_rows - 1, axis=0) + b1_ref[...]
    out = _dot(_gelu_tanh(hid).astype(MXU_DTYPE), w2_ref[...])
    row = lax.broadcasted_iota(jnp.int32, out.shape, 0)
    o_ref[0] = jnp.where(row == n_rows - 1, 0.0, out).astype(o_ref.dtype)


def _compress(raw, pos, w1, b1, w2):
    b, t, _ = raw.shape
    n_rows = t // CMP_STRIDE
    half = CMP_STRIDE
    kw = half * LANES
    eye = jnp.eye(N_GROUPS, dtype=F32)
    w1r = w1.reshape(CMP_BLOCK, HEAD_DIM, CMP_HIDDEN)

    def expand(wpart):
        return jnp.einsum("jdh,ab->jadbh", wpart, eye).reshape(kw, N_GROUPS * CMP_HIDDEN).astype(MXU_DTYPE)

    def expand_pos(ppart):
        return jnp.broadcast_to(ppart[:, None, :], (half, N_GROUPS, HEAD_DIM)).reshape(1, kw)

    wa, wb = expand(w1r[:half]), expand(w1r[half:])
    pa, pb = expand_pos(pos[:half]), expand_pos(pos[half:])
    b1e = jnp.tile(b1.reshape(1, CMP_HIDDEN), (1, N_GROUPS))
    w2e = jnp.einsum("hd,ab->ahbd", w2, eye).reshape(N_GROUPS * CMP_HIDDEN, LANES).astype(MXU_DTYPE)
    const = lambda i: (0, 0)
    return pl.pallas_call(
        _compress_kernel,
        grid=(b,),
        in_specs=[
            pl.BlockSpec((1, n_rows, kw), lambda i: (i, 0, 0)),
            pl.BlockSpec((1, kw), const),
            pl.BlockSpec((1, kw), const),
            pl.BlockSpec((kw, N_GROUPS * CMP_HIDDEN), const),
            pl.BlockSpec((kw, N_GROUPS * CMP_HIDDEN), const),
            pl.BlockSpec((1, N_GROUPS * CMP_HIDDEN), const),
            pl.BlockSpec((N_GROUPS * CMP_HIDDEN, LANES), const),
        ],
        out_specs=pl.BlockSpec((1, n_rows, LANES), lambda i: (i, 0, 0)),
        out_shape=jax.ShapeDtypeStruct((b, n_rows, LANES), MXU_DTYPE),
        compiler_params=_params("arbitrary"),
        name="compress",
    )(raw.reshape(b, n_rows, kw), pa, pb, wa, wb, b1e, w2e)


def _lane_half(shape):
    return lax.broadcasted_iota(jnp.int32, shape, len(shape) - 1) >> _BLK_SHIFT


def _alibi_cols(c0, c1, c2, c3, shape):
    lane = lax.broadcasted_iota(jnp.int32, shape, 1)
    z = jnp.zeros(shape, F32)
    return jnp.where(lane == 0, c0, jnp.where(lane == 1, c1, jnp.where(lane == 2, c2, jnp.where(lane == 3, c3, z))))


def _q_alibi(tq, t0, slope):
    t = t0 + lax.broadcasted_iota(jnp.int32, (tq, LANES), 0)
    hi = (t >> _BLK_SHIFT).astype(F32)
    lo = (t & _BLK_MASK).astype(F32)
    return _alibi_cols(slope * SEL_BLOCK * hi, slope * lo, slope * SEL_BLOCK, slope, (tq, LANES)).astype(MXU_DTYPE)


def _k_alibi(pos):
    hi = (pos >> _BLK_SHIFT).astype(F32)
    lo = (pos & _BLK_MASK).astype(F32)
    return _alibi_cols(-1.0, -1.0, hi, lo, pos.shape).astype(MXU_DTYPE)


def _build_qaug(qaug_scr, slab_of, t0, tq, other):
    half = _lane_half((tq, LANES))
    for r in range(N_REP):
        slab = slab_of(r)
        for g in range(N_GROUPS):
            fill = jnp.zeros_like(slab) if other is None else other
            qaug_scr[g, r * tq:(r + 1) * tq, 0:LANES] = jnp.where(half == g, slab, fill)
            qaug_scr[g, r * tq:(r + 1) * tq, LANES:AUG_K] = _q_alibi(tq, t0, float(_SLOPES[g, r]))


def _store_slabs(o_ref, per_group, tq):
    lane = lax.broadcasted_iota(jnp.int32, (tq, LANES), 1)
    for r in range(N_REP):
        a = per_group[0][r * tq:(r + 1) * tq]
        b = per_group[1][r * tq:(r + 1) * tq]
        o_ref[0, :, r * LANES:(r + 1) * LANES] = jnp.where(lane < HEAD_DIM, a, b).astype(o_ref.dtype)


def _cmp_kernel(q_ref, kc_ref, vc_ref, ovl_ref, o_ref, ns_ref, qaug_scr, score_scr, cnt_scr, *, tq, n_cmp, n_sel):
    i = pl.program_id(1)
    t0 = i * tq
    _build_qaug(qaug_scr, lambda r: q_ref[0, :, r * LANES:(r + 1) * LANES], t0, tq, None)

    kc = kc_ref[0]
    vc = vc_ref[0]
    half_k = _lane_half((n_cmp, LANES))
    c_pos = lax.broadcasted_iota(jnp.int32, (n_cmp, LANES), 0) * CMP_STRIDE + (CMP_BLOCK - 1)
    k_ali = _k_alibi(c_pos)
    t_row = t0 + lax.broadcasted_iota(jnp.int32, (tq, n_cmp), 0)
    c_end = lax.broadcasted_iota(jnp.int32, (tq, n_cmp), 1) * CMP_STRIDE + (CMP_BLOCK - 1)
    vis = (c_end <= t_row)[None]

    outs, psl_t = [], []
    for g in range(N_GROUPS):
        kaug = jnp.concatenate([jnp.where(half_k == g, kc, jnp.zeros_like(kc)), k_ali], axis=1)
        s = _dot_nt(qaug_scr[g], kaug).reshape(N_REP, tq, n_cmp)
        s = jnp.where(vis, s, NEG_INF)
        m = jnp.max(s, axis=-1, keepdims=True)
        e = jnp.where(vis, jnp.exp(s - m), 0.0)
        l = jnp.sum(e, axis=-1, keepdims=True)
        p = e * jnp.where(l > 0.0, 1.0 / l, 0.0)
        pb = p.astype(MXU_DTYPE).reshape(N_REP * tq, n_cmp)
        outs.append(_dot(pb, vc))
        pt = _dot_nt(ovl_ref[...], pb)
        psl_t.append(pt[:, 0:tq] + pt[:, tq:2 * tq] + pt[:, 2 * tq:3 * tq] + pt[:, 3 * tq:4 * tq])
    _store_slabs(o_ref, outs, tq)

    blk = lax.broadcasted_iota(jnp.int32, (SEL_BLOCK, tq), 0)
    cur = (t0 + lax.broadcasted_iota(jnp.int32, (SEL_BLOCK, tq), 1)) >> _BLK_SHIFT
    valid = blk <= cur
    forced = (blk == 0) | (blk == cur) | (blk == cur - 1)
    for n, g in enumerate((1, 0)):
        score_scr[n] = jnp.where(forced, FORCE_SCORE, jnp.where(valid, psl_t[g][0:SEL_BLOCK], NEG_INF))
    cnt_scr[...] = jnp.zeros_like(cnt_scr)

    n_valid = jnp.minimum(((t0 + tq - 1) >> _BLK_SHIFT) + 1, n_sel)
    trips = jnp.where(n_valid <= SEL_TOPK, 0, n_valid)

    def count(j, carry):
        for n in range(N_GROUPS):
            score = score_scr[n]
            xj = jnp.broadcast_to(score_scr[n, pl.ds(j, 1), :], (SEL_BLOCK, tq))
            beats = (xj > score) | ((xj == score) & (blk > j))
            cnt_scr[n] += jnp.where(beats, 1, 0)
        return carry
    lax.fori_loop(0, trips, count, 0)

    notsel = [jnp.where((cnt_scr[n] < SEL_TOPK) & valid, 0.0, 1.0) for n in range(N_GROUPS)]
    ns_ref[0] = jnp.transpose(jnp.concatenate(notsel, axis=0)).astype(ns_ref.dtype)


def _overlap_t(t):
    nc_pad = t // CMP_STRIDE
    ns = t // SEL_BLOCK
    c_start = np.arange(nc_pad) * CMP_STRIDE
    s_start = np.arange(ns) * SEL_BLOCK
    ov = np.clip(np.minimum(c_start[:, None] + CMP_BLOCK, s_start[None, :] + SEL_BLOCK)
                 - np.maximum(c_start[:, None], s_start[None, :]), 0, None) / CMP_BLOCK
    ov[nc_pad - 1] = 0.0
    out = np.zeros((LANES, nc_pad), np.float32)
    out[0:ns] = ov.T
    out[HEAD_DIM:HEAD_DIM + ns] = ov.T
    return out


def _cmp_attention(p16, kc, vc, *, tq=256):
    b, t, _ = p16.shape
    tq = min(tq, t)
    n_cmp = t // CMP_STRIDE
    n_sel = t // SEL_BLOCK
    assert n_sel <= SEL_BLOCK and n_cmp % LANES == 0
    ovl = jnp.asarray(_overlap_t(t), MXU_DTYPE)
    return pl.pallas_call(
        functools.partial(_cmp_kernel, tq=tq, n_cmp=n_cmp, n_sel=n_sel),
        grid=(b, t // tq),
        in_specs=[
            pl.BlockSpec((1, tq, 512), lambda bi, i: (bi, i, 0)),
            pl.BlockSpec((1, n_cmp, LANES), lambda bi, i: (bi, 0, 0)),
            pl.BlockSpec((1, n_cmp, LANES), lambda bi, i: (bi, 0, 0)),
            pl.BlockSpec((LANES, n_cmp), lambda bi, i: (0, 0)),
        ],
        out_specs=[
            pl.BlockSpec((1, tq, 512), lambda bi, i: (bi, i, 0)),
            pl.BlockSpec((1, tq, LANES), lambda bi, i: (bi, i, 0)),
        ],
        out_shape=[
            jax.ShapeDtypeStruct((b, t, 512), F32),
            jax.ShapeDtypeStruct((b, t, LANES), MXU_DTYPE),
        ],
        scratch_shapes=[
            pltpu.VMEM((N_GROUPS, N_REP * tq, AUG_K), MXU_DTYPE),
            pltpu.VMEM((N_GROUPS, SEL_BLOCK, tq), F32),
            pltpu.VMEM((N_GROUPS, SEL_BLOCK, tq), jnp.int32),
        ],
        compiler_params=_params("arbitrary", "arbitrary"),
        name="cmp_attn",
    )(p16, kc, vc, ovl)


def _build_kv(k_ref, v_ref, kaug_scr, vaug_scr, *, tile, seq, pad_tiles, select):
    half = _lane_half((tile, LANES))
    lane = lax.broadcasted_iota(jnp.int32, (tile, LANES), 1)
    for c in range(pad_tiles):
        for g in range(N_GROUPS):
            pad = jnp.where((half != g) & ((lane & _BLK_MASK) == 0), -MASK_BIG, 0.0).astype(MXU_DTYPE)
            kaug_scr[g, c * tile:(c + 1) * tile, 0:LANES] = pad
            kaug_scr[g, c * tile:(c + 1) * tile, LANES:AUG_K] = jnp.zeros((tile, LANES), MXU_DTYPE)
            vaug_scr[g, c] = jnp.zeros((LANES, tile), MXU_DTYPE)

    def body(c, carry):
        r0 = pl.multiple_of(c * tile, tile)
        k = k_ref[0, pl.ds(r0, tile), :]
        v = v_ref[0, pl.ds(r0, tile), :].astype(F32)
        pos = r0 + lax.broadcasted_iota(jnp.int32, (tile, LANES), 0)
        k_ali = _k_alibi(pos)
        if select:
            off = jnp.where((lane & _BLK_MASK) == (pos >> _BLK_SHIFT), -MASK_BIG, 0.0).astype(MXU_DTYPE)
        else:
            off = jnp.zeros((tile, LANES), MXU_DTYPE)
        dst = pl.multiple_of(r0 + pad_tiles * tile, tile)
        for g in range(N_GROUPS):
            kaug_scr[g, pl.ds(dst, tile), 0:LANES] = jnp.where(half == g, k, off)
            kaug_scr[g, pl.ds(dst, tile), LANES:AUG_K] = k_ali
            vaug_scr[g, c + pad_tiles] = jnp.transpose(jnp.where(half == g, v, 1.0)).astype(MXU_DTYPE)
        return carry
    lax.fori_loop(0, seq // tile, body, 0)


def _mask_bias(tile, off, window):
    rows = N_REP * tile
    key = lax.broadcasted_iota(jnp.int32, (tile, rows), 0)
    qry = lax.broadcasted_iota(jnp.int32, (tile, rows), 1) & (tile - 1)
    dist = qry - key + off
    keep = dist >= 0
    if window is not None:
        keep = keep & (dist < window)
    return jnp.where(keep, 0.0, NEG_INF)


def _emit_slabs(o_ref, row0, per_group, tile):
    sub = lax.broadcasted_iota(jnp.int32, (LANES, N_REP * tile), 0)
    both = jnp.where(sub < HEAD_DIM, per_group[0], per_group[1])
    for r in range(N_REP):
        o_ref[0, row0:row0 + tile, r * LANES:(r + 1) * LANES] = jnp.transpose(
            both[:, r * tile:(r + 1) * tile]).astype(o_ref.dtype)


def _sel_kernel(q_ref, ns_ref, k_ref, v_ref, o_ref, kaug_scr, vaug_scr, qaug_scr, s_a, s_b, bias_scr, acc_scr, m_scr,
                *, tile, seq):
    i = pl.program_id(1)
    t0 = i * tile

    @pl.when(i == 0)
    def _():
        _build_kv(k_ref, v_ref, kaug_scr, vaug_scr, tile=tile, seq=seq, pad_tiles=0, select=True)
        bias_scr[...] = _mask_bias(tile, 0, None)

    _build_qaug(qaug_scr, lambda r: q_ref[0, :, r * LANES:(r + 1) * LANES], t0, tile, ns_ref[0])
    acc_scr[...] = jnp.zeros_like(acc_scr)
    m_scr[...] = jnp.full_like(m_scr, -3e38)

    def scores_into(s_buf, j):
        k0 = pl.multiple_of(j * tile, tile)
        for g in range(N_GROUPS):
            s_buf[g] = _dot_nt(kaug_scr[g, pl.ds(k0, tile), :], qaug_scr[g])

    def consume(s_buf, j, diagonal):
        for g in range(N_GROUPS):
            s = s_buf[g]
            if diagonal:
                s = s + bias_scr[...]
            m_old = m_scr[g]
            m_new = jnp.maximum(m_old, jnp.max(s, axis=0, keepdims=True))
            p = jnp.exp(s - m_new).astype(MXU_DTYPE)
            acc_scr[g] = acc_scr[g] * jnp.exp(m_old - m_new) + _dot(vaug_scr[g, j], p)
            m_scr[g] = m_new

    scores_into(s_a, 0)

    def pair(jj, carry):
        scores_into(s_b, 2 * jj + 1)
        consume(s_a, 2 * jj, False)
        scores_into(s_a, 2 * jj + 2)
        consume(s_b, 2 * jj + 1, False)
        return carry
    lax.fori_loop(0, i >> 1, pair, 0)

    @pl.when((i & 1) == 0)
    def _():
        consume(s_a, i, True)

    @pl.when((i & 1) == 1)
    def _():
        scores_into(s_b, i)
        consume(s_a, i - 1, False)
        consume(s_b, i, True)

    outs = []
    for g in range(N_GROUPS):
        acc = acc_scr[g]
        den_row = (1 - g) * HEAD_DIM
        outs.append(acc / acc[den_row:den_row + 1, :])
    _emit_slabs(o_ref, 0, outs, tile)


def _sel_attention(p16, ns, q_blk, k_blk, v_blk, *, tile):
    b, t, _ = p16.shape
    tile = min(tile, t)
    rows = N_REP * tile
    return pl.pallas_call(
        functools.partial(_sel_kernel, tile=tile, seq=t),
        grid=(b, t // tile),
        in_specs=[
            pl.BlockSpec((1, tile, 512), lambda bi, i: (bi, i, q_blk)),
            pl.BlockSpec((1, tile, LANES), lambda bi, i: (bi, i, 0)),
            pl.BlockSpec((1, t, LANES), lambda bi, i: (bi, 0, k_blk)),
            pl.BlockSpec((1, t, LANES), lambda bi, i: (bi, 0, v_blk)),
        ],
        out_specs=pl.BlockSpec((1, tile, 512), lambda bi, i: (bi, i, 0)),
        out_shape=jax.ShapeDtypeStruct((b, t, 512), F32),
        scratch_shapes=[
            pltpu.VMEM((N_GROUPS, t, AUG_K), MXU_DTYPE),
            pltpu.VMEM((N_GROUPS, t // tile, LANES, tile), MXU_DTYPE),
            pltpu.VMEM((N_GROUPS, rows, AUG_K), MXU_DTYPE),
            pltpu.VMEM((N_GROUPS, tile, rows), F32),
            pltpu.VMEM((N_GROUPS, tile, rows), F32),
            pltpu.VMEM((tile, rows), F32),
            pltpu.VMEM((N_GROUPS, LANES, rows), F32),
            pltpu.VMEM((N_GROUPS, 1, rows), F32),
        ],
        compiler_params=_params("arbitrary", "arbitrary"),
        name="attn_sel",
    )(p16, ns, p16, p16)


def _band_kernel(*refs, tile, seq, n_prev, window, has_sink):
    refs = list(refs)
    sink_ref = refs.pop(0) if has_sink else None
    q_ref, qn_ref, k_ref, v_ref, o_ref, kaug_scr, vaug_scr, qaug_a, qaug_b, s_a, s_b, bias_scr = refs
    i2 = pl.program_id(1)
    n_q = seq // tile
    span = n_prev + 1
    rows = N_REP * tile
    lane = lax.broadcasted_iota(jnp.int32, (tile, LANES), 1)
    pad_flag = jnp.where((lane & _BLK_MASK) == 0, 1.0, 0.0).astype(MXU_DTYPE)
    offs = [(n_prev - n) * tile for n in range(span)]
    masked = [n for n in range(span) if offs[n] - (tile - 1) < 0 or offs[n] + tile - 1 >= window]

    def scores_into(s_buf, qaug_buf, qt):
        k0 = pl.multiple_of(qt * tile, tile)
        for g in range(N_GROUPS):
            s_buf[g] = _dot_nt(kaug_scr[g, pl.ds(k0, span * tile), :], qaug_buf[g])

    def finish(s_buf, qt, row0):
        outs = []
        for g in range(N_GROUPS):
            def piece(n):
                s = s_buf[g, n * tile:(n + 1) * tile, :]
                return s + bias_scr[masked.index(n)] if n in masked else s
            m = jnp.max(piece(0), axis=0, keepdims=True)
            for n in range(1, span):
                m = jnp.maximum(m, jnp.max(piece(n), axis=0, keepdims=True))
            pv = None
            for n in range(span):
                p = jnp.exp(piece(n) - m).astype(MXU_DTYPE)
                d = _dot(vaug_scr[g, qt + n], p)
                pv = d if pv is None else pv + d
            den_row = (1 - g) * HEAD_DIM
            den = pv[den_row:den_row + 1, :]
            if has_sink:
                col = lax.broadcasted_iota(jnp.int32, (1, rows), 1)
                sink = jnp.zeros((1, rows), F32)
                for r in range(N_REP):
                    sink = jnp.where(col >= r * tile, sink_ref[g * N_REP + r], sink)
                m2 = jnp.maximum(m, sink)
                scale = jnp.exp(m - m2)
                pv = pv * scale
                den = den * scale + jnp.exp(sink - m2)
            outs.append(pv / den)
        _emit_slabs(o_ref, row0, outs, tile)

    @pl.when(i2 == 0)
    def _():
        _build_kv(k_ref, v_ref, kaug_scr, vaug_scr, tile=tile, seq=seq, pad_tiles=n_prev, select=False)
        for idx, n in enumerate(masked):
            bias_scr[idx] = _mask_bias(tile, offs[n], window)
        _build_qaug(qaug_a, lambda r: q_ref[0, 0:tile, r * LANES:(r + 1) * LANES], 0, tile, pad_flag)
        scores_into(s_a, qaug_a, 0)

    qt = 2 * i2
    _build_qaug(qaug_b, lambda r: q_ref[0, tile:2 * tile, r * LANES:(r + 1) * LANES], (qt + 1) * tile, tile, pad_flag)
    scores_into(s_b, qaug_b, qt + 1)
    finish(s_a, qt, 0)
    qn = jnp.minimum(qt + 2, n_q - 1)
    _build_qaug(qaug_a, lambda r: qn_ref[0, :, r * LANES:(r + 1) * LANES], qn * tile, tile, pad_flag)
    scores_into(s_a, qaug_a, qn)
    finish(s_b, qt + 1, tile)


def _band_attention(p16, q_blk, k_blk, v_blk, *, tile, window, sinks=None):
    b, t, _ = p16.shape
    tile = min(tile, t // 2)
    n_q = t // tile
    n_prev = -(-(window - 1) // tile)
    span = n_prev + 1
    assert n_q % 2 == 0
    has_sink = sinks is not None
    rows = N_REP * tile
    offs = [(n_prev - n) * tile for n in range(span)]
    n_masked = sum(1 for o in offs if o - (tile - 1) < 0 or o + tile - 1 >= window)
    in_specs, args = [], []
    if has_sink:
        in_specs.append(pl.BlockSpec(memory_space=pltpu.SMEM))
        args.append(sinks.astype(F32))
    in_specs += [
        pl.BlockSpec((1, 2 * tile, 512), lambda bi, i: (bi, i, q_blk)),
        pl.BlockSpec((1, tile, 512), lambda bi, i: (bi, jnp.minimum(2 * i + 2, n_q - 1), q_blk)),
        pl.BlockSpec((1, t, LANES), lambda bi, i: (bi, 0, k_blk)),
        pl.BlockSpec((1, t, LANES), lambda bi, i: (bi, 0, v_blk)),
    ]
    args += [p16, p16, p16, p16]
    return pl.pallas_call(
        functools.partial(_band_kernel, tile=tile, seq=t, n_prev=n_prev, window=window, has_sink=has_sink),
        grid=(b, n_q // 2),
        in_specs=in_specs,
        out_specs=pl.BlockSpec((1, 2 * tile, 512), lambda bi, i: (bi, i, 0)),
        out_shape=jax.ShapeDtypeStruct((b, t, 512), F32),
        scratch_shapes=[
            pltpu.VMEM((N_GROUPS, t + n_prev * tile, AUG_K), MXU_DTYPE),
            pltpu.VMEM((N_GROUPS, n_q + n_prev, LANES, tile), MXU_DTYPE),
            pltpu.VMEM((N_GROUPS, rows, AUG_K), MXU_DTYPE),
            pltpu.VMEM((N_GROUPS, rows, AUG_K), MXU_DTYPE),
            pltpu.VMEM((N_GROUPS, span * tile, rows), F32),
            pltpu.VMEM((N_GROUPS, span * tile, rows), F32),
            pltpu.VMEM((n_masked, tile, rows), F32),
        ],
        compiler_params=_params("arbitrary", "arbitrary"),
        name="attn_band" + ("_sink" if has_sink else ""),
    )(*args)


N_BRANCH = 3
GATE_EXP_W = N_BRANCH * N_REP * LANES


def _combine_kernel(h_ref, oc_ref, os_ref, ow_ref, ob_ref, gt_ref, e_ref, w_ref, o_ref):
    sig = jax.nn.sigmoid(gt_ref[...])
    hi = sig.astype(MXU_DTYPE)
    lo = (sig - hi.astype(F32)).astype(MXU_DTYPE)
    gx = _dot(hi, e_ref[...]) + _dot(lo, e_ref[...])
    acc = h_ref[...]
    branches = (oc_ref, os_ref, ow_ref)
    for r in range(N_REP):
        sl = slice(r * LANES, (r + 1) * LANES)
        o_a = jnp.zeros_like(oc_ref[:, sl])
        for br in range(N_BRANCH):
            c = (br * N_REP + r) * LANES
            o_a = o_a + gx[:, c:c + LANES] * branches[br][:, sl]
        acc = acc + _dot(o_a.astype(MXU_DTYPE), w_ref[sl, :])
    acc = acc + _dot(ob_ref[...].astype(MXU_DTYPE), w_ref[512:1024, :])
    o_ref[...] = acc


def _gate_expand():
    e = np.zeros((LANES, GATE_EXP_W), np.float32)
    for br in range(N_BRANCH):
        for r in range(N_REP):
            for g in range(N_GROUPS):
                c0 = (br * N_REP + r) * LANES + g * HEAD_DIM
                e[br * 8 + r * 2 + g, c0:c0 + HEAD_DIM] = 1.0
    return e


def _combine(h2, o_cmp, o_slc, o_win, o_b, gates, w_out_p, *, tm=512):
    m = h2.shape[0]
    tm = min(tm, m)
    row = lambda i: (i, 0)
    const = lambda i: (0, 0)
    e = jnp.asarray(_gate_expand(), MXU_DTYPE)
    return pl.pallas_call(
        _combine_kernel,
        grid=(m // tm,),
        in_specs=[
            pl.BlockSpec((tm, D_MODEL), row),
            pl.BlockSpec((tm, 512), row),
            pl.BlockSpec((tm, 512), row),
            pl.BlockSpec((tm, 512), row),
            pl.BlockSpec((tm, 512), row),
            pl.BlockSpec((tm, LANES), row),
            pl.BlockSpec((LANES, GATE_EXP_W), const),
            pl.BlockSpec((D_MODEL, D_MODEL), const),
        ],
        out_specs=pl.BlockSpec((tm, D_MODEL), row),
        out_shape=jax.ShapeDtypeStruct((m, D_MODEL), F32),
        compiler_params=_params("arbitrary"),
        name="combine_out",
    )(h2, o_cmp, o_slc, o_win, o_b, gates, e, w_out_p)


def _slab_cols(w):
    lead = w.shape[:-1]
    return w.reshape(*lead, N_GROUPS, N_REP, HEAD_DIM).swapaxes(-3, -2).reshape(*lead, N_GROUPS * N_REP * HEAD_DIM)


def _arrange_proj_weight(w):
    q_a = _slab_cols(w[:, 0:512])
    kc, vc, ks, vs, kw, vw = [w[:, 512 + j * LANES:512 + (j + 1) * LANES] for j in range(6)]
    gates = w[:, 1280:1304].reshape(-1, N_GROUPS, N_REP, N_BRANCH).transpose(0, 3, 2, 1).reshape(-1, 24)
    gates = jnp.pad(gates, ((0, 0), (0, LANES - 24)))
    q_b = _slab_cols(w[:, 1304:1816])
    k_b, v_b = w[:, 1816:1944], w[:, 1944:2072]
    return jnp.concatenate([q_a, q_b, ks, vs, kw, vw, k_b, v_b, kc, vc, gates], axis=1).astype(MXU_DTYPE)


def _arrange_out_weight(w):
    w_a = _slab_cols(w[0:512].T).T
    w_b = _slab_cols(w[512:1024].T).T
    return jnp.concatenate([w_a, w_b], axis=0).astype(MXU_DTYPE)


_QA_BLK, _QB_BLK = 0, 1
_KS_BLK, _VS_BLK, _KW_BLK, _VW_BLK, _KB_BLK, _VB_BLK = 8, 9, 10, 11, 12, 13


def kernel(x, ffn1_norm, ffn1_w_in, ffn1_w_out, mix_norm, w_mix_in, cmp_k_pos, cmp_k_w1, cmp_k_b1, cmp_k_w2, cmp_v_pos, cmp_v_w1, cmp_v_b1, cmp_v_w2, swa_sinks, w_mix_out, ffn2_norm, ffn2_w_in, ffn2_w_out, final_norm):
    b, t, d = x.shape
    m = b * t
    depth = ffn1_norm.shape[0]
    h = x.reshape(m, d)
    for l in range(depth):
        last = l == depth - 1
        h = _ffn(h, ffn1_norm[l], ffn1_w_in[l].astype(MXU_DTYPE), ffn1_w_out[l].astype(MXU_DTYPE))
        p16, kc_raw, vc_raw, gates = _proj(h, mix_norm[l], _arrange_proj_weight(w_mix_in[l]))
        p16 = p16.reshape(b, t, P16_W)
        kc = _compress(kc_raw.reshape(b, t, LANES), cmp_k_pos[l], cmp_k_w1[l], cmp_k_b1[l], cmp_k_w2[l])
        vc = _compress(vc_raw.reshape(b, t, LANES), cmp_v_pos[l], cmp_v_w1[l], cmp_v_b1[l], cmp_v_w2[l])
        o_cmp, notsel = _cmp_attention(p16, kc, vc)
        o_slc = _sel_attention(p16, notsel, _QA_BLK, _KS_BLK, _VS_BLK, tile=256)
        o_win = _band_attention(p16, _QA_BLK, _KW_BLK, _VW_BLK, tile=256, window=NSA_WINDOW)
        o_b = _band_attention(p16, _QB_BLK, _KB_BLK, _VB_BLK, tile=128, window=SWA_WINDOW, sinks=swa_sinks[l])
        h = _combine(h, o_cmp.reshape(m, 512), o_slc.reshape(m, 512), o_win.reshape(m, 512), o_b.reshape(m, 512),
                     gates, _arrange_out_weight(w_mix_out[l]))
        h = _ffn(h, ffn2_norm[l], ffn2_w_in[l].astype(MXU_DTYPE), ffn2_w_out[l].astype(MXU_DTYPE),
                 final_norm if last else None)
    if depth == 0:
        raise ValueError("depth must be positive")
    return h.reshape(b, t, d)
```

```python
import functools

import numpy as np
import jax
import jax.numpy as jnp
from jax import lax
from jax.experimental import pallas as pl
from jax.experimental.pallas import tpu as pltpu

F32 = jnp.float32
MXU_DTYPE = jnp.bfloat16

D_MODEL = 1024
HEAD_DIM = 64
N_GROUPS = 2
N_REP = 4
CMP_BLOCK = 32
CMP_STRIDE = 16
CMP_HIDDEN = 256
SEL_BLOCK = 64
SEL_TOPK = 16
NSA_WINDOW = 512
SWA_WINDOW = 128
D_FF = 2816
NORM_EPS = 1e-6
NEG_INF = -1e30
FORCE_SCORE = 1e9
ATTN_SCALE = HEAD_DIM ** -0.5
MASK_BIG = 2.0 ** 100
LANES = 128
AUG_K = 2 * LANES
_BLK_SHIFT = 6
_BLK_MASK = SEL_BLOCK - 1
VMEM_LIMIT = 48 * 1024 * 1024

_SLOPES = (2.0 ** (-8.0 * np.arange(1, 9) / 8)).reshape(N_GROUPS, N_REP)

_NT = (((1,), (1,)), ((), ()))


def _dot(a, b):
    return jnp.dot(a, b, preferred_element_type=F32)


def _dot_nt(a, b):
    return lax.dot_general(a, b, _NT, preferred_element_type=F32)


def _rmsnorm(x, g):
    ms = jnp.mean(x * x, axis=-1, keepdims=True)
    return x * lax.rsqrt(ms + NORM_EPS) * g


def _params(*sem):
    return pltpu.CompilerParams(dimension_semantics=sem, vmem_limit_bytes=VMEM_LIMIT)


def _ffn_kernel(x_ref, g_ref, wi_ref, wo_ref, *rest, n_f, final_norm):
    if final_norm:
        fg_ref, o_ref = rest
    else:
        (o_ref,) = rest
    x = x_ref[...]
    xn = _rmsnorm(x, g_ref[...]).astype(MXU_DTYPE)
    acc = None
    for f in range(n_f):
        gate = _dot(xn, wi_ref[f])
        up = _dot(xn, wi_ref[n_f + f])
        act = gate * jax.nn.sigmoid(gate) * up
        part = _dot(act.astype(MXU_DTYPE), wo_ref[f])
        acc = part if acc is None else acc + part
    h = x + 0.5 * acc
    if final_norm:
        h = _rmsnorm(h, fg_ref[...])
    o_ref[...] = h


def _ffn(x2, norm_g, w_in, w_out, final_g=None, *, tm=512, tf=256):
    m = x2.shape[0]
    tm = min(tm, m)
    n_f = D_FF // tf
    final_norm = final_g is not None
    wi = w_in.reshape(D_MODEL, 2 * n_f, tf).transpose(1, 0, 2)
    wo = w_out.reshape(n_f, tf, D_MODEL)
    resident = pl.Buffered(1)
    in_specs = [
        pl.BlockSpec((tm, D_MODEL), lambda i: (i, 0)),
        pl.BlockSpec((1, D_MODEL), lambda i: (0, 0)),
        pl.BlockSpec((2 * n_f, D_MODEL, tf), lambda i: (0, 0, 0), pipeline_mode=resident),
        pl.BlockSpec((n_f, tf, D_MODEL), lambda i: (0, 0, 0), pipeline_mode=resident),
    ]
    args = [x2, norm_g.reshape(1, D_MODEL), wi, wo]
    if final_norm:
        in_specs.append(pl.BlockSpec((1, D_MODEL), lambda i: (0, 0)))
        args.append(final_g.reshape(1, D_MODEL))
    return pl.pallas_call(
        functools.partial(_ffn_kernel, n_f=n_f, final_norm=final_norm),
        grid=(m // tm,),
        in_specs=in_specs,
        out_specs=pl.BlockSpec((tm, D_MODEL), lambda i: (i, 0)),
        out_shape=jax.ShapeDtypeStruct((m, D_MODEL), F32),
        compiler_params=_params("arbitrary"),
        name="ffn_final" if final_norm else "ffn",
    )(*args)


P16_W = 1024 + 6 * LANES
PROJ_W = P16_W + 3 * LANES


def _proj_kernel(h_ref, g_ref, w_ref, p16_ref, kc_ref, vc_ref, gt_ref):
    xn = _rmsnorm(h_ref[...], g_ref[...]).astype(MXU_DTYPE)
    y = _dot(xn, w_ref[...])
    p16_ref[:, 0:1024] = (y[:, 0:1024] * ATTN_SCALE).astype(MXU_DTYPE)
    p16_ref[:, 1024:P16_W] = y[:, 1024:P16_W].astype(MXU_DTYPE)
    kc_ref[...] = y[:, P16_W:P16_W + LANES]
    vc_ref[...] = y[:, P16_W + LANES:P16_W + 2 * LANES]
    gt_ref[...] = y[:, P16_W + 2 * LANES:PROJ_W]


def _proj(h2, norm_g, w, *, tm=512):
    m = h2.shape[0]
    tm = min(tm, m)
    row = lambda i: (i, 0)
    return pl.pallas_call(
        _proj_kernel,
        grid=(m // tm,),
        in_specs=[
            pl.BlockSpec((tm, D_MODEL), row),
            pl.BlockSpec((1, D_MODEL), lambda i: (0, 0)),
            pl.BlockSpec((D_MODEL, PROJ_W), lambda i: (0, 0)),
        ],
        out_specs=[
            pl.BlockSpec((tm, P16_W), row),
            pl.BlockSpec((tm, LANES), row),
            pl.BlockSpec((tm, LANES), row),
            pl.BlockSpec((tm, LANES), row),
        ],
        out_shape=[
            jax.ShapeDtypeStruct((m, P16_W), MXU_DTYPE),
            jax.ShapeDtypeStruct((m, LANES), F32),
            jax.ShapeDtypeStruct((m, LANES), F32),
            jax.ShapeDtypeStruct((m, LANES), F32),
        ],
        compiler_params=_params("arbitrary"),
        name="mix_proj",
    )(h2, norm_g.reshape(1, D_MODEL), w)


def _gelu_tanh(x):
    return 0.5 * x * (1.0 + jnp.tanh(np.sqrt(2.0 / np.pi) * (x + 0.044715 * (x * x * x))))


def _compress_kernel(raw_ref, pa_ref, pb_ref, wa_ref, wb_ref, b1_ref, w2_ref, o_ref):
    x = raw_ref[0]
    n_rows = x.shape[0]
    first = _dot((x + pa_ref[...]).astype(MXU_DTYPE), wa_ref[...])
    second = _dot((x + pb_ref[...]).astype(MXU_DTYPE), wb_ref[...])
    hid = first + pltpu.roll(second, n_rows - 1, axis=0) + b1_ref[...]
    out = _dot(_gelu_tanh(hid).astype(MXU_DTYPE), w2_ref[...])
    row = lax.broadcasted_iota(jnp.int32, out.shape, 0)
    o_ref[0] = jnp.where(row == n_rows - 1, 0.0, out).astype(o_ref.dtype)


def _compress(raw, pos, w1, b1, w2):
    b, t, _ = raw.shape
    n_rows = t // CMP_STRIDE
    half = CMP_STRIDE
    kw = half * LANES
    eye = jnp.eye(N_GROUPS, dtype=F32)
    w1r = w1.reshape(CMP_BLOCK, HEAD_DIM, CMP_HIDDEN)

    def expand(wpart):
        return jnp.einsum("jdh,ab->jadbh", wpart, eye).reshape(kw, N_GROUPS * CMP_HIDDEN).astype(MXU_DTYPE)

    def expand_pos(ppart):
        return jnp.broadcast_to(ppart[:, None, :], (half, N_GROUPS, HEAD_DIM)).reshape(1, kw)

    wa, wb = expand(w1r[:half]), expand(w1r[half:])
    pa, pb = expand_pos(pos[:half]), expand_pos(pos[half:])
    b1e = jnp.tile(b1.reshape(1, CMP_HIDDEN), (1, N_GROUPS))
    w2e = jnp.einsum("hd,ab->ahbd", w2, eye).reshape(N_GROUPS * CMP_HIDDEN, LANES).astype(MXU_DTYPE)
    const = lambda i: (0, 0)
    return pl.pallas_call(
        _compress_kernel,
        grid=(b,),
        in_specs=[
            pl.BlockSpec((1, n_rows, kw), lambda i: (i, 0, 0)),
            pl.BlockSpec((1, kw), const),
            pl.BlockSpec((1, kw), const),
            pl.BlockSpec((kw, N_GROUPS * CMP_HIDDEN), const),
            pl.BlockSpec((kw, N_GROUPS * CMP_HIDDEN), const),
            pl.BlockSpec((1, N_GROUPS * CMP_HIDDEN), const),
            pl.BlockSpec((N_GROUPS * CMP_HIDDEN, LANES), const),
        ],
        out_specs=pl.BlockSpec((1, n_rows, LANES), lambda i: (i, 0, 0)),
        out_shape=jax.ShapeDtypeStruct((b, n_rows, LANES), MXU_DTYPE),
        compiler_params=_params("arbitrary"),
        name="compress",
    )(raw.reshape(b, n_rows, kw), pa, pb, wa, wb, b1e, w2e)


def _lane_half(shape):
    return lax.broadcasted_iota(jnp.int32, shape, len(shape) - 1) >> _BLK_SHIFT


def _alibi_cols(c0, c1, c2, c3, shape):
    lane = lax.broadcasted_iota(jnp.int32, shape, 1)
    z = jnp.zeros(shape, F32)
    return jnp.where(lane == 0, c0, jnp.where(lane == 1, c1, jnp.where(lane == 2, c2, jnp.where(lane == 3, c3, z))))


def _q_alibi(tq, t0, slope):
    t = t0 + lax.broadcasted_iota(jnp.int32, (tq, LANES), 0)
    hi = (t >> _BLK_SHIFT).astype(F32)
    lo = (t & _BLK_MASK).astype(F32)
    return _alibi_cols(slope * SEL_BLOCK * hi, slope * lo, slope * SEL_BLOCK, slope, (tq, LANES)).astype(MXU_DTYPE)


def _k_alibi(pos):
    hi = (pos >> _BLK_SHIFT).astype(F32)
    lo = (pos & _BLK_MASK).astype(F32)
    return _alibi_cols(-1.0, -1.0, hi, lo, pos.shape).astype(MXU_DTYPE)


def _build_qaug(qaug_scr, slab_of, t0, tq, other):
    half = _lane_half((tq, LANES))
    for r in range(N_REP):
        slab = slab_of(r)
        for g in range(N_GROUPS):
            fill = jnp.zeros_like(slab) if other is None else other
            qaug_scr[g, r * tq:(r + 1) * tq, 0:LANES] = jnp.where(half == g, slab, fill)
            qaug_scr[g, r * tq:(r + 1) * tq, LANES:AUG_K] = _q_alibi(tq, t0, float(_SLOPES[g, r]))


def _store_slabs(o_ref, per_group, tq):
    lane = lax.broadcasted_iota(jnp.int32, (tq, LANES), 1)
    for r in range(N_REP):
        a = per_group[0][r * tq:(r + 1) * tq]
        b = per_group[1][r * tq:(r + 1) * tq]
        o_ref[0, :, r * LANES:(r + 1) * LANES] = jnp.where(lane < HEAD_DIM, a, b).astype(o_ref.dtype)


def _cmp_kernel(q_ref, kc_ref, vc_ref, ovl_ref, o_ref, ns_ref, need_ref, qaug_scr, score_scr, cnt_scr, *, tq, n_cmp, n_sel):
    i = pl.program_id(1)
    t0 = i * tq
    _build_qaug(qaug_scr, lambda r: q_ref[0, :, r * LANES:(r + 1) * LANES], t0, tq, None)

    kc = kc_ref[0]
    vc = vc_ref[0]
    half_k = _lane_half((n_cmp, LANES))
    c_pos = lax.broadcasted_iota(jnp.int32, (n_cmp, LANES), 0) * CMP_STRIDE + (CMP_BLOCK - 1)
    k_ali = _k_alibi(c_pos)
    t_row = t0 + lax.broadcasted_iota(jnp.int32, (tq, n_cmp), 0)
    c_end = lax.broadcasted_iota(jnp.int32, (tq, n_cmp), 1) * CMP_STRIDE + (CMP_BLOCK - 1)
    vis = (c_end <= t_row)[None]

    outs, psl_t = [], []
    for g in range(N_GROUPS):
        kaug = jnp.concatenate([jnp.where(half_k == g, kc, jnp.zeros_like(kc)), k_ali], axis=1)
        s = _dot_nt(qaug_scr[g], kaug).reshape(N_REP, tq, n_cmp)
        s = jnp.where(vis, s, NEG_INF)
        m = jnp.max(s, axis=-1, keepdims=True)
        e = jnp.where(vis, jnp.exp(s - m), 0.0)
        l = jnp.sum(e, axis=-1, keepdims=True)
        p = e * jnp.where(l > 0.0, 1.0 / l, 0.0)
        pb = p.astype(MXU_DTYPE).reshape(N_REP * tq, n_cmp)
        outs.append(_dot(pb, vc))
        pt = _dot_nt(ovl_ref[...], pb)
        psl_t.append(pt[:, 0:tq] + pt[:, tq:2 * tq] + pt[:, 2 * tq:3 * tq] + pt[:, 3 * tq:4 * tq])
    _store_slabs(o_ref, outs, tq)

    blk = lax.broadcasted_iota(jnp.int32, (SEL_BLOCK, tq), 0)
    cur = (t0 + lax.broadcasted_iota(jnp.int32, (SEL_BLOCK, tq), 1)) >> _BLK_SHIFT
    valid = blk <= cur
    forced = (blk == 0) | (blk == cur) | (blk == cur - 1)
    for n, g in enumerate((1, 0)):
        score_scr[n] = jnp.where(forced, FORCE_SCORE, jnp.where(valid, psl_t[g][0:SEL_BLOCK], NEG_INF))
    cnt_scr[...] = jnp.zeros_like(cnt_scr)

    n_valid = jnp.minimum(((t0 + tq - 1) >> _BLK_SHIFT) + 1, n_sel)
    chunk = 8
    for c0 in range(0, n_sel, chunk):
        @pl.when((n_valid > SEL_TOPK) & (c0 < n_valid))
        def _(c0=c0):
            for n in range(N_GROUPS):
                score = score_scr[n]
                cnt = cnt_scr[n]
                for j in range(c0, min(c0 + chunk, n_sel)):
                    xj = jnp.broadcast_to(score[j:j + 1, :], (SEL_BLOCK, tq))
                    beats = (xj > score) | ((xj == score) & (blk > j))
                    cnt = cnt + jnp.where(beats, 1, 0)
                cnt_scr[n] = cnt

    notsel = [jnp.where((cnt_scr[n] < SEL_TOPK) & valid, 0.0, 1.0) for n in range(N_GROUPS)]
    ns_t = jnp.transpose(jnp.concatenate(notsel, axis=0))
    ns_ref[0] = ns_t.astype(ns_ref.dtype)
    need_ref[0, 0] = (1.0 - jnp.min(ns_t, axis=0, keepdims=True)).astype(jnp.int32)


def _overlap_t(t):
    nc_pad = t // CMP_STRIDE
    ns = t // SEL_BLOCK
    c_start = np.arange(nc_pad) * CMP_STRIDE
    s_start = np.arange(ns) * SEL_BLOCK
    ov = np.clip(np.minimum(c_start[:, None] + CMP_BLOCK, s_start[None, :] + SEL_BLOCK)
                 - np.maximum(c_start[:, None], s_start[None, :]), 0, None) / CMP_BLOCK
    ov[nc_pad - 1] = 0.0
    out = np.zeros((LANES, nc_pad), np.float32)
    out[0:ns] = ov.T
    out[HEAD_DIM:HEAD_DIM + ns] = ov.T
    return out


def _cmp_attention(p16, kc, vc, *, tq=256):
    b, t, _ = p16.shape
    tq = min(tq, t)
    n_cmp = t // CMP_STRIDE
    n_sel = t // SEL_BLOCK
    assert n_sel <= SEL_BLOCK and n_cmp % LANES == 0
    ovl = jnp.asarray(_overlap_t(t), MXU_DTYPE)
    return pl.pallas_call(
        functools.partial(_cmp_kernel, tq=tq, n_cmp=n_cmp, n_sel=n_sel),
        grid=(b, t // tq),
        in_specs=[
            pl.BlockSpec((1, tq, 512), lambda bi, i: (bi, i, 0)),
            pl.BlockSpec((1, n_cmp, LANES), lambda bi, i: (bi, 0, 0)),
            pl.BlockSpec((1, n_cmp, LANES), lambda bi, i: (bi, 0, 0)),
            pl.BlockSpec((LANES, n_cmp), lambda bi, i: (0, 0)),
        ],
        out_specs=[
            pl.BlockSpec((1, tq, 512), lambda bi, i: (bi, i, 0)),
            pl.BlockSpec((1, tq, LANES), lambda bi, i: (bi, i, 0)),
            pl.BlockSpec((1, 1, 1, LANES), lambda bi, i: (bi, i, 0, 0)),
        ],
        out_shape=[
            jax.ShapeDtypeStruct((b, t, 512), F32),
            jax.ShapeDtypeStruct((b, t, LANES), MXU_DTYPE),
            jax.ShapeDtypeStruct((b, t // tq, 1, LANES), jnp.int32),
        ],
        scratch_shapes=[
            pltpu.VMEM((N_GROUPS, N_REP * tq, AUG_K), MXU_DTYPE),
            pltpu.VMEM((N_GROUPS, SEL_BLOCK, tq), F32),
            pltpu.VMEM((N_GROUPS, SEL_BLOCK, tq), jnp.int32),
        ],
        compiler_params=_params("arbitrary", "arbitrary"),
        name="cmp_attn",
    )(p16, kc, vc, ovl)


def _build_kv(k_ref, v_ref, kaug_scr, vaug_scr, *, tile, seq, pad_tiles, select):
    half = _lane_half((tile, LANES))
    lane = lax.broadcasted_iota(jnp.int32, (tile, LANES), 1)
    for c in range(pad_tiles):
        for g in range(N_GROUPS):
            pad = jnp.where((half != g) & ((lane & _BLK_MASK) == 0), -MASK_BIG, 0.0).astype(MXU_DTYPE)
            kaug_scr[g, c * tile:(c + 1) * tile, 0:LANES] = pad
            kaug_scr[g, c * tile:(c + 1) * tile, LANES:AUG_K] = jnp.zeros((tile, LANES), MXU_DTYPE)
            vaug_scr[g, c] = jnp.zeros((LANES, tile), MXU_DTYPE)

    def body(c, carry):
        r0 = pl.multiple_of(c * tile, tile)
        k = k_ref[0, pl.ds(r0, tile), :]
        v = v_ref[0, pl.ds(r0, tile), :].astype(F32)
        pos = r0 + lax.broadcasted_iota(jnp.int32, (tile, LANES), 0)
        k_ali = _k_alibi(pos)
        if select:
            off = jnp.where((lane & _BLK_MASK) == (pos >> _BLK_SHIFT), -MASK_BIG, 0.0).astype(MXU_DTYPE)
        else:
            off = jnp.zeros((tile, LANES), MXU_DTYPE)
        dst = pl.multiple_of(r0 + pad_tiles * tile, tile)
        for g in range(N_GROUPS):
            kaug_scr[g, pl.ds(dst, tile), 0:LANES] = jnp.where(half == g, k, off)
            kaug_scr[g, pl.ds(dst, tile), LANES:AUG_K] = k_ali
            vaug_scr[g, c + pad_tiles] = jnp.transpose(jnp.where(half == g, v, 1.0)).astype(MXU_DTYPE)
        return carry
    lax.fori_loop(0, seq // tile, body, 0)


def _mask_bias(tile, off, window):
    rows = N_REP * tile
    key = lax.broadcasted_iota(jnp.int32, (tile, rows), 0)
    qry = lax.broadcasted_iota(jnp.int32, (tile, rows), 1) & (tile - 1)
    dist = qry - key + off
    keep = dist >= 0
    if window is not None:
        keep = keep & (dist < window)
    return jnp.where(keep, 0.0, NEG_INF)


def _emit_slabs(o_ref, row0, per_group, tile):
    sub = lax.broadcasted_iota(jnp.int32, (LANES, N_REP * tile), 0)
    both = jnp.where(sub < HEAD_DIM, per_group[0], per_group[1])
    for r in range(N_REP):
        o_ref[0, row0:row0 + tile, r * LANES:(r + 1) * LANES] = jnp.transpose(
            both[:, r * tile:(r + 1) * tile]).astype(o_ref.dtype)


def _sel_kernel(need_ref, q_ref, ns_ref, k_ref, v_ref, o_ref, kaug_scr, vaug_scr, qaug_scr, s_a, s_b, bias_scr, acc_scr,
                m_scr, todo_scr, *, tile, seq):
    i = pl.program_id(1)
    t0 = i * tile
    n_q = seq // tile

    base = (pl.program_id(0) * n_q + i) * n_q

    def scan(j, n):
        wanted = need_ref[base + j] > 0

        @pl.when(wanted)
        def _():
            todo_scr[n] = j
        return n + wanted.astype(jnp.int32)
    n_todo = lax.fori_loop(0, i, scan, 0)
    todo_scr[n_todo] = i

    @pl.when(i == 0)
    def _():
        _build_kv(k_ref, v_ref, kaug_scr, vaug_scr, tile=tile, seq=seq, pad_tiles=0, select=True)
        bias_scr[...] = _mask_bias(tile, 0, None)

    _build_qaug(qaug_scr, lambda r: q_ref[0, :, r * LANES:(r + 1) * LANES], t0, tile, ns_ref[0])
    acc_scr[...] = jnp.zeros_like(acc_scr)
    m_scr[...] = jnp.full_like(m_scr, -3e38)

    def scores_into(s_buf, j):
        k0 = pl.multiple_of(j * tile, tile)
        for g in range(N_GROUPS):
            s_buf[g] = _dot_nt(kaug_scr[g, pl.ds(k0, tile), :], qaug_scr[g])

    def consume(s_buf, j, diagonal):
        for g in range(N_GROUPS):
            s = s_buf[g]
            if diagonal:
                s = s + bias_scr[...]
            m_old = m_scr[g]
            m_new = jnp.maximum(m_old, jnp.max(s, axis=0, keepdims=True))
            p = jnp.exp(s - m_new).astype(MXU_DTYPE)
            acc_scr[g] = acc_scr[g] * jnp.exp(m_old - m_new) + _dot(vaug_scr[g, j], p)
            m_scr[g] = m_new

    scores_into(s_a, todo_scr[0])

    def pair(jj, carry):
        first, second = todo_scr[2 * jj], todo_scr[2 * jj + 1]
        scores_into(s_b, second)
        consume(s_a, first, False)
        scores_into(s_a, todo_scr[2 * jj + 2])
        consume(s_b, second, False)
        return carry
    lax.fori_loop(0, n_todo >> 1, pair, 0)

    @pl.when((n_todo & 1) == 0)
    def _():
        consume(s_a, i, True)

    @pl.when((n_todo & 1) == 1)
    def _():
        scores_into(s_b, i)
        consume(s_a, todo_scr[n_todo - 1], False)
        consume(s_b, i, True)

    outs = []
    for g in range(N_GROUPS):
        acc = acc_scr[g]
        den_row = (1 - g) * HEAD_DIM
        outs.append(acc / acc[den_row:den_row + 1, :])
    _emit_slabs(o_ref, 0, outs, tile)


def _sel_attention(p16, ns, block_need, q_blk, k_blk, v_blk, *, tile):
    b, t, _ = p16.shape
    tile = min(tile, t)
    n_q = t // tile
    rows = N_REP * tile
    assert block_need.shape[1] == n_q
    per_block = jnp.maximum(block_need[:, :, 0, 0:SEL_BLOCK], block_need[:, :, 0, SEL_BLOCK:])[:, :, 0:t // SEL_BLOCK]
    tile_need = per_block.reshape(b, n_q, n_q, tile // SEL_BLOCK).max(axis=-1).reshape(-1)
    grid_spec = pltpu.PrefetchScalarGridSpec(
        num_scalar_prefetch=1,
        grid=(b, n_q),
        in_specs=[
            pl.BlockSpec((1, tile, 512), lambda bi, i, need: (bi, i, q_blk)),
            pl.BlockSpec((1, tile, LANES), lambda bi, i, need: (bi, i, 0)),
            pl.BlockSpec((1, t, LANES), lambda bi, i, need: (bi, 0, k_blk)),
            pl.BlockSpec((1, t, LANES), lambda bi, i, need: (bi, 0, v_blk)),
        ],
        out_specs=pl.BlockSpec((1, tile, 512), lambda bi, i, need: (bi, i, 0)),
        scratch_shapes=[
            pltpu.VMEM((N_GROUPS, t, AUG_K), MXU_DTYPE),
            pltpu.VMEM((N_GROUPS, t // tile, LANES, tile), MXU_DTYPE),
            pltpu.VMEM((N_GROUPS, rows, AUG_K), MXU_DTYPE),
            pltpu.VMEM((N_GROUPS, tile, rows), F32),
            pltpu.VMEM((N_GROUPS, tile, rows), F32),
            pltpu.VMEM((tile, rows), F32),
            pltpu.VMEM((N_GROUPS, LANES, rows), F32),
            pltpu.VMEM((N_GROUPS, 1, rows), F32),
            pltpu.SMEM((n_q + 1,), jnp.int32),
        ],
    )
    return pl.pallas_call(
        functools.partial(_sel_kernel, tile=tile, seq=t),
        grid_spec=grid_spec,
        out_shape=jax.ShapeDtypeStruct((b, t, 512), F32),
        compiler_params=_params("arbitrary", "arbitrary"),
        name="attn_sel",
    )(tile_need, p16, ns, p16, p16)


def _band_kernel(*refs, tile, seq, n_prev, window, has_sink):
    refs = list(refs)
    sink_ref = refs.pop(0) if has_sink else None
    q_ref, qn_ref, k_ref, v_ref, o_ref, kaug_scr, vaug_scr, qaug_a, qaug_b, s_a, s_b, bias_scr = refs
    i2 = pl.program_id(1)
    n_q = seq // tile
    span = n_prev + 1
    rows = N_REP * tile
    lane = lax.broadcasted_iota(jnp.int32, (tile, LANES), 1)
    pad_flag = jnp.where((lane & _BLK_MASK) == 0, 1.0, 0.0).astype(MXU_DTYPE)
    offs = [(n_prev - n) * tile for n in range(span)]
    masked = [n for n in range(span) if offs[n] - (tile - 1) < 0 or offs[n] + tile - 1 >= window]

    def scores_into(s_buf, qaug_buf, qt):
        k0 = pl.multiple_of(qt * tile, tile)
        for g in range(N_GROUPS):
            s_buf[g] = _dot_nt(kaug_scr[g, pl.ds(k0, span * tile), :], qaug_buf[g])

    def finish(s_buf, qt, row0):
        outs = []
        for g in range(N_GROUPS):
            def piece(n):
                s = s_buf[g, n * tile:(n + 1) * tile, :]
                return s + bias_scr[masked.index(n)] if n in masked else s
            m = jnp.max(piece(0), axis=0, keepdims=True)
            for n in range(1, span):
                m = jnp.maximum(m, jnp.max(piece(n), axis=0, keepdims=True))
            pv = None
            for n in range(span):
                p = jnp.exp(piece(n) - m).astype(MXU_DTYPE)
                d = _dot(vaug_scr[g, qt + n], p)
                pv = d if pv is None else pv + d
            den_row = (1 - g) * HEAD_DIM
            den = pv[den_row:den_row + 1, :]
            if has_sink:
                col = lax.broadcasted_iota(jnp.int32, (1, rows), 1)
                sink = jnp.zeros((1, rows), F32)
                for r in range(N_REP):
                    sink = jnp.where(col >= r * tile, sink_ref[g * N_REP + r], sink)
                m2 = jnp.maximum(m, sink)
                scale = jnp.exp(m - m2)
                pv = pv * scale
                den = den * scale + jnp.exp(sink - m2)
            outs.append(pv / den)
        _emit_slabs(o_ref, row0, outs, tile)

    @pl.when(i2 == 0)
    def _():
        _build_kv(k_ref, v_ref, kaug_scr, vaug_scr, tile=tile, seq=seq, pad_tiles=n_prev, select=False)
        for idx, n in enumerate(masked):
            bias_scr[idx] = _mask_bias(tile, offs[n], window)
        _build_qaug(qaug_a, lambda r: q_ref[0, 0:tile, r * LANES:(r + 1) * LANES], 0, tile, pad_flag)
        scores_into(s_a, qaug_a, 0)

    qt = 2 * i2
    _build_qaug(qaug_b, lambda r: q_ref[0, tile:2 * tile, r * LANES:(r + 1) * LANES], (qt + 1) * tile, tile, pad_flag)
    scores_into(s_b, qaug_b, qt + 1)
    finish(s_a, qt, 0)
    qn = jnp.minimum(qt + 2, n_q - 1)
    _build_qaug(qaug_a, lambda r: qn_ref[0, :, r * LANES:(r + 1) * LANES], qn * tile, tile, pad_flag)
    scores_into(s_a, qaug_a, qn)
    finish(s_b, qt + 1, tile)


def _band_attention(p16, q_blk, k_blk, v_blk, *, tile, window, sinks=None):
    b, t, _ = p16.shape
    tile = min(tile, t // 2)
    n_q = t // tile
    n_prev = -(-(window - 1) // tile)
    span = n_prev + 1
    assert n_q % 2 == 0
    has_sink = sinks is not None
    rows = N_REP * tile
    offs = [(n_prev - n) * tile for n in range(span)]
    n_masked = sum(1 for o in offs if o - (tile - 1) < 0 or o + tile - 1 >= window)
    in_specs, args = [], []
    if has_sink:
        in_specs.append(pl.BlockSpec(memory_space=pltpu.SMEM))
        args.append(sinks.astype(F32))
    in_specs += [
        pl.BlockSpec((1, 2 * tile, 512), lambda bi, i: (bi, i, q_blk)),
        pl.BlockSpec((1, tile, 512), lambda bi, i: (bi, jnp.minimum(2 * i + 2, n_q - 1), q_blk)),
        pl.BlockSpec((1, t, LANES), lambda bi, i: (bi, 0, k_blk)),
        pl.BlockSpec((1, t, LANES), lambda bi, i: (bi, 0, v_blk)),
    ]
    args += [p16, p16, p16, p16]
    return pl.pallas_call(
        functools.partial(_band_kernel, tile=tile, seq=t, n_prev=n_prev, window=window, has_sink=has_sink),
        grid=(b, n_q // 2),
        in_specs=in_specs,
        out_specs=pl.BlockSpec((1, 2 * tile, 512), lambda bi, i: (bi, i, 0)),
        out_shape=jax.ShapeDtypeStruct((b, t, 512), F32),
        scratch_shapes=[
            pltpu.VMEM((N_GROUPS, t + n_prev * tile, AUG_K), MXU_DTYPE),
            pltpu.VMEM((N_GROUPS, n_q + n_prev, LANES, tile), MXU_DTYPE),
            pltpu.VMEM((N_GROUPS, rows, AUG_K), MXU_DTYPE),
            pltpu.VMEM((N_GROUPS, rows, AUG_K), MXU_DTYPE),
            pltpu.VMEM((N_GROUPS, span * tile, rows), F32),
            pltpu.VMEM((N_GROUPS, span * tile, rows), F32),
            pltpu.VMEM((n_masked, tile, rows), F32),
        ],
        compiler_params=_params("arbitrary", "arbitrary"),
        name="attn_band" + ("_sink" if has_sink else ""),
    )(*args)


N_BRANCH = 3
GATE_EXP_W = N_BRANCH * N_REP * LANES


def _combine_kernel(h_ref, oc_ref, os_ref, ow_ref, ob_ref, gt_ref, e_ref, w_ref, o_ref):
    sig = jax.nn.sigmoid(gt_ref[...])
    hi = sig.astype(MXU_DTYPE)
    lo = (sig - hi.astype(F32)).astype(MXU_DTYPE)
    gx = _dot(hi, e_ref[...]) + _dot(lo, e_ref[...])
    acc = h_ref[...]
    branches = (oc_ref, os_ref, ow_ref)
    for r in range(N_REP):
        sl = slice(r * LANES, (r + 1) * LANES)
        o_a = jnp.zeros_like(oc_ref[:, sl])
        for br in range(N_BRANCH):
            c = (br * N_REP + r) * LANES
            o_a = o_a + gx[:, c:c + LANES] * branches[br][:, sl]
        acc = acc + _dot(o_a.astype(MXU_DTYPE), w_ref[sl, :])
    acc = acc + _dot(ob_ref[...].astype(MXU_DTYPE), w_ref[512:1024, :])
    o_ref[...] = acc


def _gate_expand():
    e = np.zeros((LANES, GATE_EXP_W), np.float32)
    for br in range(N_BRANCH):
        for r in range(N_REP):
            for g in range(N_GROUPS):
                c0 = (br * N_REP + r) * LANES + g * HEAD_DIM
                e[br * 8 + r * 2 + g, c0:c0 + HEAD_DIM] = 1.0
    return e


def _combine(h2, o_cmp, o_slc, o_win, o_b, gates, w_out_p, *, tm=512):
    m = h2.shape[0]
    tm = min(tm, m)
    row = lambda i: (i, 0)
    const = lambda i: (0, 0)
    e = jnp.asarray(_gate_expand(), MXU_DTYPE)
    return pl.pallas_call(
        _combine_kernel,
        grid=(m // tm,),
        in_specs=[
            pl.BlockSpec((tm, D_MODEL), row),
            pl.BlockSpec((tm, 512), row),
            pl.BlockSpec((tm, 512), row),
            pl.BlockSpec((tm, 512), row),
            pl.BlockSpec((tm, 512), row),
            pl.BlockSpec((tm, LANES), row),
            pl.BlockSpec((LANES, GATE_EXP_W), const),
            pl.BlockSpec((D_MODEL, D_MODEL), const),
        ],
        out_specs=pl.BlockSpec((tm, D_MODEL), row),
        out_shape=jax.ShapeDtypeStruct((m, D_MODEL), F32),
        compiler_params=_params("arbitrary"),
        name="combine_out",
    )(h2, o_cmp, o_slc, o_win, o_b, gates, e, w_out_p)


def _slab_cols(w):
    lead = w.shape[:-1]
    return w.reshape(*lead, N_GROUPS, N_REP, HEAD_DIM).swapaxes(-3, -2).reshape(*lead, N_GROUPS * N_REP * HEAD_DIM)


def _arrange_proj_weight(w):
    q_a = _slab_cols(w[:, 0:512])
    kc, vc, ks, vs, kw, vw = [w[:, 512 + j * LANES:512 + (j + 1) * LANES] for j in range(6)]
    gates = w[:, 1280:1304].reshape(-1, N_GROUPS, N_REP, N_BRANCH).transpose(0, 3, 2, 1).reshape(-1, 24)
    gates = jnp.pad(gates, ((0, 0), (0, LANES - 24)))
    q_b = _slab_cols(w[:, 1304:1816])
    k_b, v_b = w[:, 1816:1944], w[:, 1944:2072]
    return jnp.concatenate([q_a, q_b, ks, vs, kw, vw, k_b, v_b, kc, vc, gates], axis=1).astype(MXU_DTYPE)


def _arrange_out_weight(w):
    w_a = _slab_cols(w[0:512].T).T
    w_b = _slab_cols(w[512:1024].T).T
    return jnp.concatenate([w_a, w_b], axis=0).astype(MXU_DTYPE)


_QA_BLK, _QB_BLK = 0, 1
_KS_BLK, _VS_BLK, _KW_BLK, _VW_BLK, _KB_BLK, _VB_BLK = 8, 9, 10, 11, 12, 13


def kernel(x, ffn1_norm, ffn1_w_in, ffn1_w_out, mix_norm, w_mix_in, cmp_k_pos, cmp_k_w1, cmp_k_b1, cmp_k_w2, cmp_v_pos, cmp_v_w1, cmp_v_b1, cmp_v_w2, swa_sinks, w_mix_out, ffn2_norm, ffn2_w_in, ffn2_w_out, final_norm):
    b, t, d = x.shape
    m = b * t
    depth = ffn1_norm.shape[0]
    h = x.reshape(m, d)
    for l in range(depth):
        last = l == depth - 1
        h = _ffn(h, ffn1_norm[l], ffn1_w_in[l].astype(MXU_DTYPE), ffn1_w_out[l].astype(MXU_DTYPE))
        p16, kc_raw, vc_raw, gates = _proj(h, mix_norm[l], _arrange_proj_weight(w_mix_in[l]))
        p16 = p16.reshape(b, t, P16_W)
        kc = _compress(kc_raw.reshape(b, t, LANES), cmp_k_pos[l], cmp_k_w1[l], cmp_k_b1[l], cmp_k_w2[l])
        vc = _compress(vc_raw.reshape(b, t, LANES), cmp_v_pos[l], cmp_v_w1[l], cmp_v_b1[l], cmp_v_w2[l])
        o_cmp, notsel, block_need = _cmp_attention(p16, kc, vc)
        o_slc = _sel_attention(p16, notsel, block_need, _QA_BLK, _KS_BLK, _VS_BLK, tile=256)
        o_win = _band_attention(p16, _QA_BLK, _KW_BLK, _VW_BLK, tile=256, window=NSA_WINDOW)
        o_b = _band_attention(p16, _QB_BLK, _KB_BLK, _VB_BLK, tile=128, window=SWA_WINDOW, sinks=swa_sinks[l])
        h = _combine(h, o_cmp.reshape(m, 512), o_slc.reshape(m, 512), o_win.reshape(m, 512), o_b.reshape(m, 512),
                     gates, _arrange_out_weight(w_mix_out[l]))
        h = _ffn(h, ffn2_norm[l], ffn2_w_in[l].astype(MXU_DTYPE), ffn2_w_out[l].astype(MXU_DTYPE),
                 final_norm if last else None)
    if depth == 0:
        raise ValueError("depth must be positive")
    return h.reshape(b, t, d)
```

```python
import functools

import numpy as np
import jax
import jax.numpy as jnp
from jax import lax
from jax.experimental import pallas as pl
from jax.experimental.pallas import tpu as pltpu

F32 = jnp.float32
MXU_DTYPE = jnp.bfloat16
BRANCH_DTYPE = jnp.bfloat16

D_MODEL = 1024
HEAD_DIM = 64
N_GROUPS = 2
N_REP = 4
CMP_BLOCK = 32
CMP_STRIDE = 16
CMP_HIDDEN = 256
SEL_BLOCK = 64
SEL_TOPK = 16
NSA_WINDOW = 512
SWA_WINDOW = 128
D_FF = 2816
NORM_EPS = 1e-6
NEG_INF = -1e30
FORCE_SCORE = 1e9
ATTN_SCALE = HEAD_DIM ** -0.5
MASK_BIG = 2.0 ** 100
LANES = 128
AUG_K = 2 * LANES
_BLK_SHIFT = 6
_BLK_MASK = SEL_BLOCK - 1
VMEM_LIMIT = 48 * 1024 * 1024

_SLOPES = (2.0 ** (-8.0 * np.arange(1, 9) / 8)).reshape(N_GROUPS, N_REP)

_NT = (((1,), (1,)), ((), ()))


def _dot(a, b):
    return jnp.dot(a, b, preferred_element_type=F32)


def _dot_nt(a, b):
    return lax.dot_general(a, b, _NT, preferred_element_type=F32)


def _rmsnorm(x, g):
    ms = jnp.mean(x * x, axis=-1, keepdims=True)
    return x * lax.rsqrt(ms + NORM_EPS) * g


def _params(*sem):
    return pltpu.CompilerParams(dimension_semantics=sem, vmem_limit_bytes=VMEM_LIMIT)


def _ffn_kernel(x_ref, g_ref, wi_ref, wo_ref, *rest, n_f, final_norm):
    if final_norm:
        fg_ref, o_ref = rest
    else:
        (o_ref,) = rest
    x = x_ref[...]
    xn = _rmsnorm(x, g_ref[...]).astype(MXU_DTYPE)
    acc = None
    for f in range(n_f):
        gate = _dot(xn, wi_ref[f])
        up = _dot(xn, wi_ref[n_f + f])
        act = gate * jax.nn.sigmoid(gate) * up
        part = _dot(act.astype(MXU_DTYPE), wo_ref[f])
        acc = part if acc is None else acc + part
    h = x + 0.5 * acc
    if final_norm:
        h = _rmsnorm(h, fg_ref[...])
    o_ref[...] = h


def _ffn(x2, norm_g, w_in, w_out, final_g=None, *, tm=512, tf=256):
    m = x2.shape[0]
    tm = min(tm, m)
    n_f = D_FF // tf
    final_norm = final_g is not None
    wi = w_in.reshape(D_MODEL, 2 * n_f, tf).transpose(1, 0, 2)
    wo = w_out.reshape(n_f, tf, D_MODEL)
    resident = pl.Buffered(1)
    in_specs = [
        pl.BlockSpec((tm, D_MODEL), lambda i: (i, 0)),
        pl.BlockSpec((1, D_MODEL), lambda i: (0, 0)),
        pl.BlockSpec((2 * n_f, D_MODEL, tf), lambda i: (0, 0, 0), pipeline_mode=resident),
        pl.BlockSpec((n_f, tf, D_MODEL), lambda i: (0, 0, 0), pipeline_mode=resident),
    ]
    args = [x2, norm_g.reshape(1, D_MODEL), wi, wo]
    if final_norm:
        in_specs.append(pl.BlockSpec((1, D_MODEL), lambda i: (0, 0)))
        args.append(final_g.reshape(1, D_MODEL))
    return pl.pallas_call(
        functools.partial(_ffn_kernel, n_f=n_f, final_norm=final_norm),
        grid=(m // tm,),
        in_specs=in_specs,
        out_specs=pl.BlockSpec((tm, D_MODEL), lambda i: (i, 0)),
        out_shape=jax.ShapeDtypeStruct((m, D_MODEL), F32),
        compiler_params=_params("arbitrary"),
        name="ffn_final" if final_norm else "ffn",
    )(*args)


P16_W = 1024 + 6 * LANES
PROJ_W = P16_W + 3 * LANES


def _proj_kernel(h_ref, g_ref, w_ref, p16_ref, kc_ref, vc_ref, gt_ref):
    xn = _rmsnorm(h_ref[...], g_ref[...]).astype(MXU_DTYPE)
    y = _dot(xn, w_ref[...])
    p16_ref[:, 0:1024] = (y[:, 0:1024] * ATTN_SCALE).astype(MXU_DTYPE)
    p16_ref[:, 1024:P16_W] = y[:, 1024:P16_W].astype(MXU_DTYPE)
    kc_ref[...] = y[:, P16_W:P16_W + LANES]
    vc_ref[...] = y[:, P16_W + LANES:P16_W + 2 * LANES]
    gt_ref[...] = y[:, P16_W + 2 * LANES:PROJ_W]


def _proj(h2, norm_g, w, *, tm=512):
    m = h2.shape[0]
    tm = min(tm, m)
    row = lambda i: (i, 0)
    return pl.pallas_call(
        _proj_kernel,
        grid=(m // tm,),
        in_specs=[
            pl.BlockSpec((tm, D_MODEL), row),
            pl.BlockSpec((1, D_MODEL), lambda i: (0, 0)),
            pl.BlockSpec((D_MODEL, PROJ_W), lambda i: (0, 0)),
        ],
        out_specs=[
            pl.BlockSpec((tm, P16_W), row),
            pl.BlockSpec((tm, LANES), row),
            pl.BlockSpec((tm, LANES), row),
            pl.BlockSpec((tm, LANES), row),
        ],
        out_shape=[
            jax.ShapeDtypeStruct((m, P16_W), MXU_DTYPE),
            jax.ShapeDtypeStruct((m, LANES), F32),
            jax.ShapeDtypeStruct((m, LANES), F32),
            jax.ShapeDtypeStruct((m, LANES), F32),
        ],
        compiler_params=_params("arbitrary"),
        name="mix_proj",
    )(h2, norm_g.reshape(1, D_MODEL), w)


def _gelu_tanh(x):
    return 0.5 * x * (1.0 + jnp.tanh(np.sqrt(2.0 / np.pi) * (x + 0.044715 * (x * x * x))))


def _compress_kernel(raw_ref, pa_ref, pb_ref, wa_ref, wb_ref, b1_ref, w2_ref, o_ref):
    x = raw_ref[0]
    n_rows = x.shape[0]
    first = _dot((x + pa_ref[...]).astype(MXU_DTYPE), wa_ref[...])
    second = _dot((x + pb_ref[...]).astype(MXU_DTYPE), wb_ref[...])
    hid = first + pltpu.roll(second, n_rows - 1, axis=0) + b1_ref[...]
    out = _dot(_gelu_tanh(hid).astype(MXU_DTYPE), w2_ref[...])
    row = lax.broadcasted_iota(jnp.int32, out.shape, 0)
    o_ref[0] = jnp.where(row == n_rows - 1, 0.0, out).astype(o_ref.dtype)


def _compress(raw, pos, w1, b1, w2):
    b, t, _ = raw.shape
    n_rows = t // CMP_STRIDE
    half = CMP_STRIDE
    kw = half * LANES
    eye = jnp.eye(N_GROUPS, dtype=F32)
    w1r = w1.reshape(CMP_BLOCK, HEAD_DIM, CMP_HIDDEN)

    def expand(wpart):
        return jnp.einsum("jdh,ab->jadbh", wpart, eye).reshape(kw, N_GROUPS * CMP_HIDDEN).astype(MXU_DTYPE)

    def expand_pos(ppart):
        return jnp.broadcast_to(ppart[:, None, :], (half, N_GROUPS, HEAD_DIM)).reshape(1, kw)

    wa, wb = expand(w1r[:half]), expand(w1r[half:])
    pa, pb = expand_pos(pos[:half]), expand_pos(pos[half:])
    b1e = jnp.tile(b1.reshape(1, CMP_HIDDEN), (1, N_GROUPS))
    w2e = jnp.einsum("hd,ab->ahbd", w2, eye).reshape(N_GROUPS * CMP_HIDDEN, LANES).astype(MXU_DTYPE)
    const = lambda i: (0, 0)
    return pl.pallas_call(
        _compress_kernel,
        grid=(b,),
        in_specs=[
            pl.BlockSpec((1, n_rows, kw), lambda i: (i, 0, 0)),
            pl.BlockSpec((1, kw), const),
            pl.BlockSpec((1, kw), const),
            pl.BlockSpec((kw, N_GROUPS * CMP_HIDDEN), const),
            pl.BlockSpec((kw, N_GROUPS * CMP_HIDDEN), const),
            pl.BlockSpec((1, N_GROUPS * CMP_HIDDEN), const),
            pl.BlockSpec((N_GROUPS * CMP_HIDDEN, LANES), const),
        ],
        out_specs=pl.BlockSpec((1, n_rows, LANES), lambda i: (i, 0, 0)),
        out_shape=jax.ShapeDtypeStruct((b, n_rows, LANES), MXU_DTYPE),
        compiler_params=_params("arbitrary"),
        name="compress",
    )(raw.reshape(b, n_rows, kw), pa, pb, wa, wb, b1e, w2e)


def _lane_half(shape):
    return lax.broadcasted_iota(jnp.int32, shape, len(shape) - 1) >> _BLK_SHIFT


def _alibi_cols(c0, c1, c2, c3, shape):
    lane = lax.broadcasted_iota(jnp.int32, shape, 1)
    z = jnp.zeros(shape, F32)
    return jnp.where(lane == 0, c0, jnp.where(lane == 1, c1, jnp.where(lane == 2, c2, jnp.where(lane == 3, c3, z))))


def _q_alibi(tq, t0, slope):
    t = t0 + lax.broadcasted_iota(jnp.int32, (tq, LANES), 0)
    hi = (t >> _BLK_SHIFT).astype(F32)
    lo = (t & _BLK_MASK).astype(F32)
    return _alibi_cols(slope * SEL_BLOCK * hi, slope * lo, slope * SEL_BLOCK, slope, (tq, LANES)).astype(MXU_DTYPE)


def _k_alibi(pos):
    hi = (pos >> _BLK_SHIFT).astype(F32)
    lo = (pos & _BLK_MASK).astype(F32)
    return _alibi_cols(-1.0, -1.0, hi, lo, pos.shape).astype(MXU_DTYPE)


def _build_qaug(qaug_scr, slab_of, t0, tq, other):
    half = _lane_half((tq, LANES))
    for r in range(N_REP):
        slab = slab_of(r)
        for g in range(N_GROUPS):
            fill = jnp.zeros_like(slab) if other is None else other
            qaug_scr[g, r * tq:(r + 1) * tq, 0:LANES] = jnp.where(half == g, slab, fill)
            qaug_scr[g, r * tq:(r + 1) * tq, LANES:AUG_K] = _q_alibi(tq, t0, float(_SLOPES[g, r]))


def _store_slabs(o_ref, per_group, tq):
    lane = lax.broadcasted_iota(jnp.int32, (tq, LANES), 1)
    for r in range(N_REP):
        a = per_group[0][r * tq:(r + 1) * tq]
        b = per_group[1][r * tq:(r + 1) * tq]
        o_ref[0, :, r * LANES:(r + 1) * LANES] = jnp.where(lane < HEAD_DIM, a, b).astype(o_ref.dtype)


def _cmp_kernel(q_ref, kc_ref, vc_ref, ovl_ref, o_ref, ns_ref, need_ref, qaug_scr, score_scr, cnt_scr, *, tq, n_cmp, n_sel):
    i = pl.program_id(1)
    t0 = i * tq
    _build_qaug(qaug_scr, lambda r: q_ref[0, :, r * LANES:(r + 1) * LANES], t0, tq, None)

    kc = kc_ref[0]
    vc = vc_ref[0]
    half_k = _lane_half((n_cmp, LANES))
    c_pos = lax.broadcasted_iota(jnp.int32, (n_cmp, LANES), 0) * CMP_STRIDE + (CMP_BLOCK - 1)
    k_ali = _k_alibi(c_pos)
    t_row = t0 + lax.broadcasted_iota(jnp.int32, (tq, n_cmp), 0)
    c_end = lax.broadcasted_iota(jnp.int32, (tq, n_cmp), 1) * CMP_STRIDE + (CMP_BLOCK - 1)
    vis = (c_end <= t_row)[None]

    outs, psl_t = [], []
    for g in range(N_GROUPS):
        kaug = jnp.concatenate([jnp.where(half_k == g, kc, jnp.zeros_like(kc)), k_ali], axis=1)
        s = _dot_nt(qaug_scr[g], kaug).reshape(N_REP, tq, n_cmp)
        s = jnp.where(vis, s, NEG_INF)
        m = jnp.max(s, axis=-1, keepdims=True)
        e = jnp.where(vis, jnp.exp(s - m), 0.0)
        l = jnp.sum(e, axis=-1, keepdims=True)
        p = e * jnp.where(l > 0.0, 1.0 / l, 0.0)
        pb = p.astype(MXU_DTYPE).reshape(N_REP * tq, n_cmp)
        outs.append(_dot(pb, vc))
        pt = _dot_nt(ovl_ref[...], pb)
        psl_t.append(pt[:, 0:tq] + pt[:, tq:2 * tq] + pt[:, 2 * tq:3 * tq] + pt[:, 3 * tq:4 * tq])
    _store_slabs(o_ref, outs, tq)

    blk = lax.broadcasted_iota(jnp.int32, (SEL_BLOCK, tq), 0)
    cur = (t0 + lax.broadcasted_iota(jnp.int32, (SEL_BLOCK, tq), 1)) >> _BLK_SHIFT
    valid = blk <= cur
    forced = (blk == 0) | (blk == cur) | (blk == cur - 1)
    for n, g in enumerate((1, 0)):
        score_scr[n] = jnp.where(forced, FORCE_SCORE, jnp.where(valid, psl_t[g][0:SEL_BLOCK], NEG_INF))
    cnt_scr[...] = jnp.zeros_like(cnt_scr)

    n_valid = jnp.minimum(((t0 + tq - 1) >> _BLK_SHIFT) + 1, n_sel)
    chunk = 8
    for c0 in range(0, n_sel, chunk):
        @pl.when((n_valid > SEL_TOPK) & (c0 < n_valid))
        def _(c0=c0):
            for n in range(N_GROUPS):
                score = score_scr[n]
                cnt = cnt_scr[n]
                for j in range(c0, min(c0 + chunk, n_sel)):
                    xj = jnp.broadcast_to(score[j:j + 1, :], (SEL_BLOCK, tq))
                    beats = (xj > score) | ((xj == score) & (blk > j))
                    cnt = cnt + jnp.where(beats, 1, 0)
                cnt_scr[n] = cnt

    notsel = [jnp.where((cnt_scr[n] < SEL_TOPK) & valid, 0.0, 1.0) for n in range(N_GROUPS)]
    ns_t = jnp.transpose(jnp.concatenate(notsel, axis=0))
    ns_ref[0] = ns_t.astype(ns_ref.dtype)
    need_ref[0, 0] = (1.0 - jnp.min(ns_t, axis=0, keepdims=True)).astype(jnp.int32)


def _overlap_t(t):
    nc_pad = t // CMP_STRIDE
    ns = t // SEL_BLOCK
    c_start = np.arange(nc_pad) * CMP_STRIDE
    s_start = np.arange(ns) * SEL_BLOCK
    ov = np.clip(np.minimum(c_start[:, None] + CMP_BLOCK, s_start[None, :] + SEL_BLOCK)
                 - np.maximum(c_start[:, None], s_start[None, :]), 0, None) / CMP_BLOCK
    ov[nc_pad - 1] = 0.0
    out = np.zeros((LANES, nc_pad), np.float32)
    out[0:ns] = ov.T
    out[HEAD_DIM:HEAD_DIM + ns] = ov.T
    return out


def _cmp_attention(p16, kc, vc, *, tq=256):
    b, t, _ = p16.shape
    tq = min(tq, t)
    n_cmp = t // CMP_STRIDE
    n_sel = t // SEL_BLOCK
    assert n_sel <= SEL_BLOCK and n_cmp % LANES == 0
    ovl = jnp.asarray(_overlap_t(t), MXU_DTYPE)
    return pl.pallas_call(
        functools.partial(_cmp_kernel, tq=tq, n_cmp=n_cmp, n_sel=n_sel),
        grid=(b, t // tq),
        in_specs=[
            pl.BlockSpec((1, tq, 512), lambda bi, i: (bi, i, 0)),
            pl.BlockSpec((1, n_cmp, LANES), lambda bi, i: (bi, 0, 0)),
            pl.BlockSpec((1, n_cmp, LANES), lambda bi, i: (bi, 0, 0)),
            pl.BlockSpec((LANES, n_cmp), lambda bi, i: (0, 0)),
        ],
        out_specs=[
            pl.BlockSpec((1, tq, 512), lambda bi, i: (bi, i, 0)),
            pl.BlockSpec((1, tq, LANES), lambda bi, i: (bi, i, 0)),
            pl.BlockSpec((1, 1, 1, LANES), lambda bi, i: (bi, i, 0, 0)),
        ],
        out_shape=[
            jax.ShapeDtypeStruct((b, t, 512), BRANCH_DTYPE),
            jax.ShapeDtypeStruct((b, t, LANES), MXU_DTYPE),
            jax.ShapeDtypeStruct((b, t // tq, 1, LANES), jnp.int32),
        ],
        scratch_shapes=[
            pltpu.VMEM((N_GROUPS, N_REP * tq, AUG_K), MXU_DTYPE),
            pltpu.VMEM((N_GROUPS, SEL_BLOCK, tq), F32),
            pltpu.VMEM((N_GROUPS, SEL_BLOCK, tq), jnp.int32),
        ],
        compiler_params=_params("arbitrary", "arbitrary"),
        name="cmp_attn",
    )(p16, kc, vc, ovl)


def _build_kv(k_ref, v_ref, kaug_scr, vaug_scr, *, tile, seq, pad_tiles, select):
    half = _lane_half((tile, LANES))
    lane = lax.broadcasted_iota(jnp.int32, (tile, LANES), 1)
    for c in range(pad_tiles):
        for g in range(N_GROUPS):
            pad = jnp.where((half != g) & ((lane & _BLK_MASK) == 0), -MASK_BIG, 0.0).astype(MXU_DTYPE)
            kaug_scr[g, c * tile:(c + 1) * tile, 0:LANES] = pad
            kaug_scr[g, c * tile:(c + 1) * tile, LANES:AUG_K] = jnp.zeros((tile, LANES), MXU_DTYPE)
            vaug_scr[g, c] = jnp.zeros((LANES, tile), MXU_DTYPE)

    def body(c, carry):
        r0 = pl.multiple_of(c * tile, tile)
        k = k_ref[0, pl.ds(r0, tile), :]
        v = v_ref[0, pl.ds(r0, tile), :].astype(F32)
        pos = r0 + lax.broadcasted_iota(jnp.int32, (tile, LANES), 0)
        k_ali = _k_alibi(pos)
        if select:
            off = jnp.where((lane & _BLK_MASK) == (pos >> _BLK_SHIFT), -MASK_BIG, 0.0).astype(MXU_DTYPE)
        else:
            off = jnp.zeros((tile, LANES), MXU_DTYPE)
        dst = pl.multiple_of(r0 + pad_tiles * tile, tile)
        for g in range(N_GROUPS):
            kaug_scr[g, pl.ds(dst, tile), 0:LANES] = jnp.where(half == g, k, off)
            kaug_scr[g, pl.ds(dst, tile), LANES:AUG_K] = k_ali
            vaug_scr[g, c + pad_tiles] = jnp.transpose(jnp.where(half == g, v, 1.0)).astype(MXU_DTYPE)
        return carry
    lax.fori_loop(0, seq // tile, body, 0)


def _mask_bias(tile, off, window):
    rows = N_REP * tile
    key = lax.broadcasted_iota(jnp.int32, (tile, rows), 0)
    qry = lax.broadcasted_iota(jnp.int32, (tile, rows), 1) & (tile - 1)
    dist = qry - key + off
    keep = dist >= 0
    if window is not None:
        keep = keep & (dist < window)
    return jnp.where(keep, 0.0, NEG_INF)


def _emit_slabs(o_ref, row0, per_group, tile):
    sub = lax.broadcasted_iota(jnp.int32, (LANES, N_REP * tile), 0)
    both = jnp.where(sub < HEAD_DIM, per_group[0], per_group[1])
    for r in range(N_REP):
        o_ref[0, row0:row0 + tile, r * LANES:(r + 1) * LANES] = jnp.transpose(
            both[:, r * tile:(r + 1) * tile]).astype(o_ref.dtype)


def _sel_kernel(need_ref, q_ref, ns_ref, k_ref, v_ref, o_ref, kaug_scr, vaug_scr, qaug_scr, s_a, s_b, bias_scr, acc_scr,
                m_scr, todo_scr, *, tile, seq):
    i = pl.program_id(1)
    t0 = i * tile
    n_q = seq // tile

    base = (pl.program_id(0) * n_q + i) * n_q

    def scan(j, n):
        wanted = need_ref[base + j] > 0

        @pl.when(wanted)
        def _():
            todo_scr[n] = j
        return n + wanted.astype(jnp.int32)
    n_todo = lax.fori_loop(0, i, scan, 0)
    todo_scr[n_todo] = i

    @pl.when(i == 0)
    def _():
        _build_kv(k_ref, v_ref, kaug_scr, vaug_scr, tile=tile, seq=seq, pad_tiles=0, select=True)
        bias_scr[...] = _mask_bias(tile, 0, None)

    _build_qaug(qaug_scr, lambda r: q_ref[0, :, r * LANES:(r + 1) * LANES], t0, tile, ns_ref[0])
    acc_scr[...] = jnp.zeros_like(acc_scr)
    m_scr[...] = jnp.full_like(m_scr, -3e38)

    def scores_into(s_buf, j):
        k0 = pl.multiple_of(j * tile, tile)
        for g in range(N_GROUPS):
            s_buf[g] = _dot_nt(kaug_scr[g, pl.ds(k0, tile), :], qaug_scr[g])

    def consume(s_buf, j, diagonal):
        for g in range(N_GROUPS):
            s = s_buf[g]
            if diagonal:
                s = s + bias_scr[...]
            m_old = m_scr[g]
            m_new = jnp.maximum(m_old, jnp.max(s, axis=0, keepdims=True))
            p = jnp.exp(s - m_new).astype(MXU_DTYPE)
            acc_scr[g] = acc_scr[g] * jnp.exp(m_old - m_new) + _dot(vaug_scr[g, j], p)
            m_scr[g] = m_new

    scores_into(s_a, todo_scr[0])

    def pair(jj, carry):
        first, second = todo_scr[2 * jj], todo_scr[2 * jj + 1]
        scores_into(s_b, second)
        consume(s_a, first, False)
        scores_into(s_a, todo_scr[2 * jj + 2])
        consume(s_b, second, False)
        return carry
    lax.fori_loop(0, n_todo >> 1, pair, 0)

    @pl.when((n_todo & 1) == 0)
    def _():
        consume(s_a, i, True)

    @pl.when((n_todo & 1) == 1)
    def _():
        scores_into(s_b, i)
        consume(s_a, todo_scr[n_todo - 1], False)
        consume(s_b, i, True)

    outs = []
    for g in range(N_GROUPS):
        acc = acc_scr[g]
        den_row = (1 - g) * HEAD_DIM
        outs.append(acc / acc[den_row:den_row + 1, :])
    _emit_slabs(o_ref, 0, outs, tile)


def _sel_attention(p16, ns, block_need, q_blk, k_blk, v_blk, *, tile):
    b, t, _ = p16.shape
    tile = min(tile, t)
    n_q = t // tile
    rows = N_REP * tile
    assert block_need.shape[1] == n_q
    per_block = jnp.maximum(block_need[:, :, 0, 0:SEL_BLOCK], block_need[:, :, 0, SEL_BLOCK:])[:, :, 0:t // SEL_BLOCK]
    tile_need = per_block.reshape(b, n_q, n_q, tile // SEL_BLOCK).max(axis=-1).reshape(-1)
    grid_spec = pltpu.PrefetchScalarGridSpec(
        num_scalar_prefetch=1,
        grid=(b, n_q),
        in_specs=[
            pl.BlockSpec((1, tile, 512), lambda bi, i, need: (bi, i, q_blk)),
            pl.BlockSpec((1, tile, LANES), lambda bi, i, need: (bi, i, 0)),
            pl.BlockSpec((1, t, LANES), lambda bi, i, need: (bi, 0, k_blk)),
            pl.BlockSpec((1, t, LANES), lambda bi, i, need: (bi, 0, v_blk)),
        ],
        out_specs=pl.BlockSpec((1, tile, 512), lambda bi, i, need: (bi, i, 0)),
        scratch_shapes=[
            pltpu.VMEM((N_GROUPS, t, AUG_K), MXU_DTYPE),
            pltpu.VMEM((N_GROUPS, t // tile, LANES, tile), MXU_DTYPE),
            pltpu.VMEM((N_GROUPS, rows, AUG_K), MXU_DTYPE),
            pltpu.VMEM((N_GROUPS, tile, rows), F32),
            pltpu.VMEM((N_GROUPS, tile, rows), F32),
            pltpu.VMEM((tile, rows), F32),
            pltpu.VMEM((N_GROUPS, LANES, rows), F32),
            pltpu.VMEM((N_GROUPS, 1, rows), F32),
            pltpu.SMEM((n_q + 1,), jnp.int32),
        ],
    )
    return pl.pallas_call(
        functools.partial(_sel_kernel, tile=tile, seq=t),
        grid_spec=grid_spec,
        out_shape=jax.ShapeDtypeStruct((b, t, 512), BRANCH_DTYPE),
        compiler_params=_params("arbitrary", "arbitrary"),
        name="attn_sel",
    )(tile_need, p16, ns, p16, p16)


def _band_kernel(*refs, tile, seq, n_prev, window, has_sink):
    refs = list(refs)
    sink_ref = refs.pop(0) if has_sink else None
    q_ref, qn_ref, k_ref, v_ref, o_ref, kaug_scr, vaug_scr, qaug_a, qaug_b, s_a, s_b, bias_scr = refs
    i2 = pl.program_id(1)
    n_q = seq // tile
    span = n_prev + 1
    rows = N_REP * tile
    lane = lax.broadcasted_iota(jnp.int32, (tile, LANES), 1)
    pad_flag = jnp.where((lane & _BLK_MASK) == 0, 1.0, 0.0).astype(MXU_DTYPE)
    offs = [(n_prev - n) * tile for n in range(span)]
    masked = [n for n in range(span) if offs[n] - (tile - 1) < 0 or offs[n] + tile - 1 >= window]

    def scores_into(s_buf, qaug_buf, qt):
        k0 = pl.multiple_of(qt * tile, tile)
        for g in range(N_GROUPS):
            s = _dot_nt(kaug_scr[g, pl.ds(k0, span * tile), :], qaug_buf[g])
            for n in range(span):
                part = s[n * tile:(n + 1) * tile]
                s_buf[g, n * tile:(n + 1) * tile, :] = part + bias_scr[masked.index(n)] if n in masked else part

    def finish(s_buf, qt, row0):
        outs = []
        for g in range(N_GROUPS):
            def piece(n):
                return s_buf[g, n * tile:(n + 1) * tile, :]
            m = jnp.max(piece(0), axis=0, keepdims=True)
            for n in range(1, span):
                m = jnp.maximum(m, jnp.max(piece(n), axis=0, keepdims=True))
            pv = None
            for n in range(span):
                p = jnp.exp(piece(n) - m).astype(MXU_DTYPE)
                d = _dot(vaug_scr[g, qt + n], p)
                pv = d if pv is None else pv + d
            den_row = (1 - g) * HEAD_DIM
            den = pv[den_row:den_row + 1, :]
            if has_sink:
                col = lax.broadcasted_iota(jnp.int32, (1, rows), 1)
                sink = jnp.zeros((1, rows), F32)
                for r in range(N_REP):
                    sink = jnp.where(col >= r * tile, sink_ref[g * N_REP + r], sink)
                m2 = jnp.maximum(m, sink)
                scale = jnp.exp(m - m2)
                pv = pv * scale
                den = den * scale + jnp.exp(sink - m2)
            outs.append(pv / den)
        _emit_slabs(o_ref, row0, outs, tile)

    @pl.when(i2 == 0)
    def _():
        _build_kv(k_ref, v_ref, kaug_scr, vaug_scr, tile=tile, seq=seq, pad_tiles=n_prev, select=False)
        for idx, n in enumerate(masked):
            bias_scr[idx] = _mask_bias(tile, offs[n], window)
        _build_qaug(qaug_a, lambda r: q_ref[0, 0:tile, r * LANES:(r + 1) * LANES], 0, tile, pad_flag)
        scores_into(s_a, qaug_a, 0)

    qt = 2 * i2
    _build_qaug(qaug_b, lambda r: q_ref[0, tile:2 * tile, r * LANES:(r + 1) * LANES], (qt + 1) * tile, tile, pad_flag)
    scores_into(s_b, qaug_b, qt + 1)
    finish(s_a, qt, 0)
    qn = jnp.minimum(qt + 2, n_q - 1)
    _build_qaug(qaug_a, lambda r: qn_ref[0, :, r * LANES:(r + 1) * LANES], qn * tile, tile, pad_flag)
    scores_into(s_a, qaug_a, qn)
    finish(s_b, qt + 1, tile)


def _band_attention(p16, q_blk, k_blk, v_blk, *, tile, window, sinks=None):
    b, t, _ = p16.shape
    tile = min(tile, t // 2)
    n_q = t // tile
    n_prev = -(-(window - 1) // tile)
    span = n_prev + 1
    assert n_q % 2 == 0
    has_sink = sinks is not None
    rows = N_REP * tile
    offs = [(n_prev - n) * tile for n in range(span)]
    n_masked = sum(1 for o in offs if o - (tile - 1) < 0 or o + tile - 1 >= window)
    in_specs, args = [], []
    if has_sink:
        in_specs.append(pl.BlockSpec(memory_space=pltpu.SMEM))
        args.append(sinks.astype(F32))
    in_specs += [
        pl.BlockSpec((1, 2 * tile, 512), lambda bi, i: (bi, i, q_blk)),
        pl.BlockSpec((1, tile, 512), lambda bi, i: (bi, jnp.minimum(2 * i + 2, n_q - 1), q_blk)),
        pl.BlockSpec((1, t, LANES), lambda bi, i: (bi, 0, k_blk)),
        pl.BlockSpec((1, t, LANES), lambda bi, i: (bi, 0, v_blk)),
    ]
    args += [p16, p16, p16, p16]
    return pl.pallas_call(
        functools.partial(_band_kernel, tile=tile, seq=t, n_prev=n_prev, window=window, has_sink=has_sink),
        grid=(b, n_q // 2),
        in_specs=in_specs,
        out_specs=pl.BlockSpec((1, 2 * tile, 512), lambda bi, i: (bi, i, 0)),
        out_shape=jax.ShapeDtypeStruct((b, t, 512), BRANCH_DTYPE),
        scratch_shapes=[
            pltpu.VMEM((N_GROUPS, t + n_prev * tile, AUG_K), MXU_DTYPE),
            pltpu.VMEM((N_GROUPS, n_q + n_prev, LANES, tile), MXU_DTYPE),
            pltpu.VMEM((N_GROUPS, rows, AUG_K), MXU_DTYPE),
            pltpu.VMEM((N_GROUPS, rows, AUG_K), MXU_DTYPE),
            pltpu.VMEM((N_GROUPS, span * tile, rows), F32),
            pltpu.VMEM((N_GROUPS, span * tile, rows), F32),
            pltpu.VMEM((n_masked, tile, rows), F32),
        ],
        compiler_params=_params("arbitrary", "arbitrary"),
        name="attn_band" + ("_sink" if has_sink else ""),
    )(*args)


N_BRANCH = 3
GATE_EXP_W = N_BRANCH * N_REP * LANES


def _combine_kernel(h_ref, oc_ref, os_ref, ow_ref, ob_ref, gt_ref, e_ref, w_ref, o_ref):
    sig = jax.nn.sigmoid(gt_ref[...])
    hi = sig.astype(MXU_DTYPE)
    lo = (sig - hi.astype(F32)).astype(MXU_DTYPE)
    gx = _dot(hi, e_ref[...]) + _dot(lo, e_ref[...])
    acc = h_ref[...]
    branches = (oc_ref, os_ref, ow_ref)
    for r in range(N_REP):
        sl = slice(r * LANES, (r + 1) * LANES)
        o_a = jnp.zeros_like(oc_ref[:, sl])
        for br in range(N_BRANCH):
            c = (br * N_REP + r) * LANES
            o_a = o_a + gx[:, c:c + LANES] * branches[br][:, sl]
        acc = acc + _dot(o_a.astype(MXU_DTYPE), w_ref[sl, :])
    acc = acc + _dot(ob_ref[...].astype(MXU_DTYPE), w_ref[512:1024, :])
    o_ref[...] = acc


def _gate_expand():
    e = np.zeros((LANES, GATE_EXP_W), np.float32)
    for br in range(N_BRANCH):
        for r in range(N_REP):
            for g in range(N_GROUPS):
                c0 = (br * N_REP + r) * LANES + g * HEAD_DIM
                e[br * 8 + r * 2 + g, c0:c0 + HEAD_DIM] = 1.0
    return e


def _combine(h2, o_cmp, o_slc, o_win, o_b, gates, w_out_p, *, tm=512):
    m = h2.shape[0]
    tm = min(tm, m)
    row = lambda i: (i, 0)
    const = lambda i: (0, 0)
    e = jnp.asarray(_gate_expand(), MXU_DTYPE)
    return pl.pallas_call(
        _combine_kernel,
        grid=(m // tm,),
        in_specs=[
            pl.BlockSpec((tm, D_MODEL), row),
            pl.BlockSpec((tm, 512), row),
            pl.BlockSpec((tm, 512), row),
            pl.BlockSpec((tm, 512), row),
            pl.BlockSpec((tm, 512), row),
            pl.BlockSpec((tm, LANES), row),
            pl.BlockSpec((LANES, GATE_EXP_W), const),
            pl.BlockSpec((D_MODEL, D_MODEL), const),
        ],
        out_specs=pl.BlockSpec((tm, D_MODEL), row),
        out_shape=jax.ShapeDtypeStruct((m, D_MODEL), F32),
        compiler_params=_params("arbitrary"),
        name="combine_out",
    )(h2, o_cmp, o_slc, o_win, o_b, gates, e, w_out_p)


def _slab_cols(w):
    lead = w.shape[:-1]
    return w.reshape(*lead, N_GROUPS, N_REP, HEAD_DIM).swapaxes(-3, -2).reshape(*lead, N_GROUPS * N_REP * HEAD_DIM)


def _arrange_proj_weight(w):
    q_a = _slab_cols(w[:, 0:512])
    kc, vc, ks, vs, kw, vw = [w[:, 512 + j * LANES:512 + (j + 1) * LANES] for j in range(6)]
    gates = w[:, 1280:1304].reshape(-1, N_GROUPS, N_REP, N_BRANCH).transpose(0, 3, 2, 1).reshape(-1, 24)
    gates = jnp.pad(gates, ((0, 0), (0, LANES - 24)))
    q_b = _slab_cols(w[:, 1304:1816])
    k_b, v_b = w[:, 1816:1944], w[:, 1944:2072]
    return jnp.concatenate([q_a, q_b, ks, vs, kw, vw, k_b, v_b, kc, vc, gates], axis=1).astype(MXU_DTYPE)


def _arrange_out_weight(w):
    w_a = _slab_cols(w[0:512].T).T
    w_b = _slab_cols(w[512:1024].T).T
    return jnp.concatenate([w_a, w_b], axis=0).astype(MXU_DTYPE)


_QA_BLK, _QB_BLK = 0, 1
_KS_BLK, _VS_BLK, _KW_BLK, _VW_BLK, _KB_BLK, _VB_BLK = 8, 9, 10, 11, 12, 13


def kernel(x, ffn1_norm, ffn1_w_in, ffn1_w_out, mix_norm, w_mix_in, cmp_k_pos, cmp_k_w1, cmp_k_b1, cmp_k_w2, cmp_v_pos, cmp_v_w1, cmp_v_b1, cmp_v_w2, swa_sinks, w_mix_out, ffn2_norm, ffn2_w_in, ffn2_w_out, final_norm):
    b, t, d = x.shape
    m = b * t
    depth = ffn1_norm.shape[0]
    h = x.reshape(m, d)
    for l in range(depth):
        last = l == depth - 1
        h = _ffn(h, ffn1_norm[l], ffn1_w_in[l].astype(MXU_DTYPE), ffn1_w_out[l].astype(MXU_DTYPE))
        p16, kc_raw, vc_raw, gates = _proj(h, mix_norm[l], _arrange_proj_weight(w_mix_in[l]))
        p16 = p16.reshape(b, t, P16_W)
        kc = _compress(kc_raw.reshape(b, t, LANES), cmp_k_pos[l], cmp_k_w1[l], cmp_k_b1[l], cmp_k_w2[l])
        vc = _compress(vc_raw.reshape(b, t, LANES), cmp_v_pos[l], cmp_v_w1[l], cmp_v_b1[l], cmp_v_w2[l])
        o_cmp, notsel, block_need = _cmp_attention(p16, kc, vc)
        o_slc = _sel_attention(p16, notsel, block_need, _QA_BLK, _KS_BLK, _VS_BLK, tile=256)
        o_win = _band_attention(p16, _QA_BLK, _KW_BLK, _VW_BLK, tile=256, window=NSA_WINDOW)
        o_b = _band_attention(p16, _QB_BLK, _KB_BLK, _VB_BLK, tile=128, window=SWA_WINDOW, sinks=swa_sinks[l])
        h = _combine(h, o_cmp.reshape(m, 512), o_slc.reshape(m, 512), o_win.reshape(m, 512), o_b.reshape(m, 512),
                     gates, _arrange_out_weight(w_mix_out[l]))
        h = _ffn(h, ffn2_norm[l], ffn2_w_in[l].astype(MXU_DTYPE), ffn2_w_out[l].astype(MXU_DTYPE),
                 final_norm if last else None)
    if depth == 0:
        raise ValueError("depth must be positive")
    return h.reshape(b, t, d)
```

```python
import functools

import numpy as np
import jax
import jax.numpy as jnp
from jax import lax
from jax.experimental import pallas as pl
from jax.experimental.pallas import tpu as pltpu

F32 = jnp.float32
MXU_DTYPE = jnp.bfloat16
BRANCH_DTYPE = jnp.bfloat16

D_MODEL = 1024
HEAD_DIM = 64
N_GROUPS = 2
N_REP = 4
CMP_BLOCK = 32
CMP_STRIDE = 16
CMP_HIDDEN = 256
SEL_BLOCK = 64
SEL_TOPK = 16
NSA_WINDOW = 512
SWA_WINDOW = 128
D_FF = 2816
NORM_EPS = 1e-6
NEG_INF = -1e30
FORCE_SCORE = 1e9
ATTN_SCALE = HEAD_DIM ** -0.5
MASK_BIG = 2.0 ** 100
LANES = 128
AUG_K = 2 * LANES
_BLK_SHIFT = 6
_BLK_MASK = SEL_BLOCK - 1
VMEM_LIMIT = 48 * 1024 * 1024

_SLOPES = (2.0 ** (-8.0 * np.arange(1, 9) / 8)).reshape(N_GROUPS, N_REP)

_NT = (((1,), (1,)), ((), ()))


def _dot(a, b):
    return jnp.dot(a, b, preferred_element_type=F32)


def _dot_nt(a, b):
    return lax.dot_general(a, b, _NT, preferred_element_type=F32)


def _rmsnorm(x, g):
    ms = jnp.mean(x * x, axis=-1, keepdims=True)
    return x * lax.rsqrt(ms + NORM_EPS) * g


def _params(*sem):
    return pltpu.CompilerParams(dimension_semantics=sem, vmem_limit_bytes=VMEM_LIMIT)


FFN_CHUNK = 256
FFN_ROWS = 512


def _swiglu_half_step(x, g_ref, wi_ref, wo_ref):
    xn = _rmsnorm(x, g_ref[...]).astype(MXU_DTYPE)
    acc = None
    for f in range(D_FF // FFN_CHUNK):
        cols = slice(f * FFN_CHUNK, (f + 1) * FFN_CHUNK)
        gate = _dot(xn, wi_ref[:, cols])
        up = _dot(xn, wi_ref[:, D_FF + f * FFN_CHUNK:D_FF + (f + 1) * FFN_CHUNK])
        act = gate * jax.nn.sigmoid(gate) * up
        part = _dot(act.astype(MXU_DTYPE), wo_ref[cols, :])
        acc = part if acc is None else acc + part
    return x + 0.5 * acc


P16_W = 1024 + 6 * LANES
PROJ_W = P16_W + 3 * LANES


def _ffn_proj_kernel(x_ref, g_ref, wi_ref, wo_ref, pg_ref, pw_ref, h_ref, p16_ref, kc_ref, vc_ref, gt_ref):
    h = _swiglu_half_step(x_ref[...], g_ref, wi_ref, wo_ref)
    h_ref[...] = h
    y = _dot(_rmsnorm(h, pg_ref[...]).astype(MXU_DTYPE), pw_ref[...])
    p16_ref[:, 0:1024] = (y[:, 0:1024] * ATTN_SCALE).astype(MXU_DTYPE)
    p16_ref[:, 1024:P16_W] = y[:, 1024:P16_W].astype(MXU_DTYPE)
    kc_ref[...] = y[:, P16_W:P16_W + LANES]
    vc_ref[...] = y[:, P16_W + LANES:P16_W + 2 * LANES]
    gt_ref[...] = y[:, P16_W + 2 * LANES:PROJ_W]


def _resident(shape):
    return pl.BlockSpec(shape, lambda i: (0,) * len(shape), pipeline_mode=pl.Buffered(1))


def _ffn_proj(x2, norm_g, w_in, w_out, proj_g, proj_w):
    m = x2.shape[0]
    tm = min(FFN_ROWS, m)
    row = lambda i: (i, 0)
    return pl.pallas_call(
        _ffn_proj_kernel,
        grid=(m // tm,),
        in_specs=[
            pl.BlockSpec((tm, D_MODEL), row),
            _resident((1, D_MODEL)),
            _resident((D_MODEL, 2 * D_FF)),
            _resident((D_FF, D_MODEL)),
            _resident((1, D_MODEL)),
            _resident((D_MODEL, PROJ_W)),
        ],
        out_specs=[
            pl.BlockSpec((tm, D_MODEL), row),
            pl.BlockSpec((tm, P16_W), row),
            pl.BlockSpec((tm, LANES), row),
            pl.BlockSpec((tm, LANES), row),
            pl.BlockSpec((tm, LANES), row),
        ],
        out_shape=[
            jax.ShapeDtypeStruct((m, D_MODEL), F32),
            jax.ShapeDtypeStruct((m, P16_W), MXU_DTYPE),
            jax.ShapeDtypeStruct((m, LANES), F32),
            jax.ShapeDtypeStruct((m, LANES), F32),
            jax.ShapeDtypeStruct((m, LANES), F32),
        ],
        compiler_params=_params("arbitrary"),
        name="ffn_proj",
    )(x2, norm_g.reshape(1, D_MODEL), w_in, w_out, proj_g.reshape(1, D_MODEL), proj_w)


N_BRANCH = 3
GATE_EXP_W = N_BRANCH * N_REP * LANES


def _combine_ffn_kernel(h_ref, oc_ref, os_ref, ow_ref, ob_ref, gt_ref, e_ref, w_ref, g_ref, wi_ref, wo_ref, *rest,
                        final_norm):
    if final_norm:
        fg_ref, o_ref = rest
    else:
        (o_ref,) = rest
    sig = jax.nn.sigmoid(gt_ref[...])
    hi = sig.astype(MXU_DTYPE)
    lo = (sig - hi.astype(F32)).astype(MXU_DTYPE)
    gx = _dot(hi, e_ref[...]) + _dot(lo, e_ref[...])
    acc = h_ref[...]
    branches = (oc_ref, os_ref, ow_ref)
    for r in range(N_REP):
        sl = slice(r * LANES, (r + 1) * LANES)
        o_a = None
        for br in range(N_BRANCH):
            c = (br * N_REP + r) * LANES
            term = gx[:, c:c + LANES] * branches[br][:, sl]
            o_a = term if o_a is None else o_a + term
        acc = acc + _dot(o_a.astype(MXU_DTYPE), w_ref[sl, :])
    acc = acc + _dot(ob_ref[...].astype(MXU_DTYPE), w_ref[512:1024, :])
    h = _swiglu_half_step(acc, g_ref, wi_ref, wo_ref)
    if final_norm:
        h = _rmsnorm(h, fg_ref[...])
    o_ref[...] = h


def _gate_expand():
    e = np.zeros((LANES, GATE_EXP_W), np.float32)
    for br in range(N_BRANCH):
        for r in range(N_REP):
            for g in range(N_GROUPS):
                c0 = (br * N_REP + r) * LANES + g * HEAD_DIM
                e[br * 8 + r * 2 + g, c0:c0 + HEAD_DIM] = 1.0
    return e


def _combine_ffn(h2, o_cmp, o_slc, o_win, o_b, gates, w_out_p, norm_g, w_in, w_out, final_g=None):
    m = h2.shape[0]
    tm = min(FFN_ROWS, m)
    row = lambda i: (i, 0)
    final_norm = final_g is not None
    e = jnp.asarray(_gate_expand(), MXU_DTYPE)
    in_specs = [
        pl.BlockSpec((tm, D_MODEL), row),
        pl.BlockSpec((tm, 512), row),
        pl.BlockSpec((tm, 512), row),
        pl.BlockSpec((tm, 512), row),
        pl.BlockSpec((tm, 512), row),
        pl.BlockSpec((tm, LANES), row),
        _resident((LANES, GATE_EXP_W)),
        _resident((D_MODEL, D_MODEL)),
        _resident((1, D_MODEL)),
        _resident((D_MODEL, 2 * D_FF)),
        _resident((D_FF, D_MODEL)),
    ]
    args = [h2, o_cmp, o_slc, o_win, o_b, gates, e, w_out_p, norm_g.reshape(1, D_MODEL), w_in, w_out]
    if final_norm:
        in_specs.append(_resident((1, D_MODEL)))
        args.append(final_g.reshape(1, D_MODEL))
    return pl.pallas_call(
        functools.partial(_combine_ffn_kernel, final_norm=final_norm),
        grid=(m // tm,),
        in_specs=in_specs,
        out_specs=pl.BlockSpec((tm, D_MODEL), row),
        out_shape=jax.ShapeDtypeStruct((m, D_MODEL), F32),
        compiler_params=_params("arbitrary"),
        name="combine_ffn_final" if final_norm else "combine_ffn",
    )(*args)


def _gelu_tanh(x):
    return 0.5 * x * (1.0 + jnp.tanh(np.sqrt(2.0 / np.pi) * (x + 0.044715 * (x * x * x))))


def _compress_kernel(raw_ref, pa_ref, pb_ref, wa_ref, wb_ref, b1_ref, w2_ref, o_ref):
    x = raw_ref[0]
    n_rows = x.shape[0]
    first = _dot((x + pa_ref[...]).astype(MXU_DTYPE), wa_ref[...])
    second = _dot((x + pb_ref[...]).astype(MXU_DTYPE), wb_ref[...])
    hid = first + pltpu.roll(second, n_rows - 1, axis=0) + b1_ref[...]
    out = _dot(_gelu_tanh(hid).astype(MXU_DTYPE), w2_ref[...])
    row = lax.broadcasted_iota(jnp.int32, out.shape, 0)
    o_ref[0] = jnp.where(row == n_rows - 1, 0.0, out).astype(o_ref.dtype)


def _compress(raw, pos, w1, b1, w2):
    b, t, _ = raw.shape
    n_rows = t // CMP_STRIDE
    half = CMP_STRIDE
    kw = half * LANES
    eye = jnp.eye(N_GROUPS, dtype=F32)
    w1r = w1.reshape(CMP_BLOCK, HEAD_DIM, CMP_HIDDEN)

    def expand(wpart):
        return jnp.einsum("jdh,ab->jadbh", wpart, eye).reshape(kw, N_GROUPS * CMP_HIDDEN).astype(MXU_DTYPE)

    def expand_pos(ppart):
        return jnp.broadcast_to(ppart[:, None, :], (half, N_GROUPS, HEAD_DIM)).reshape(1, kw)

    wa, wb = expand(w1r[:half]), expand(w1r[half:])
    pa, pb = expand_pos(pos[:half]), expand_pos(pos[half:])
    b1e = jnp.tile(b1.reshape(1, CMP_HIDDEN), (1, N_GROUPS))
    w2e = jnp.einsum("hd,ab->ahbd", w2, eye).reshape(N_GROUPS * CMP_HIDDEN, LANES).astype(MXU_DTYPE)
    const = lambda i: (0, 0)
    return pl.pallas_call(
        _compress_kernel,
        grid=(b,),
        in_specs=[
            pl.BlockSpec((1, n_rows, kw), lambda i: (i, 0, 0)),
            pl.BlockSpec((1, kw), const),
            pl.BlockSpec((1, kw), const),
            pl.BlockSpec((kw, N_GROUPS * CMP_HIDDEN), const),
            pl.BlockSpec((kw, N_GROUPS * CMP_HIDDEN), const),
            pl.BlockSpec((1, N_GROUPS * CMP_HIDDEN), const),
            pl.BlockSpec((N_GROUPS * CMP_HIDDEN, LANES), const),
        ],
        out_specs=pl.BlockSpec((1, n_rows, LANES), lambda i: (i, 0, 0)),
        out_shape=jax.ShapeDtypeStruct((b, n_rows, LANES), MXU_DTYPE),
        compiler_params=_params("arbitrary"),
        name="compress",
    )(raw.reshape(b, n_rows, kw), pa, pb, wa, wb, b1e, w2e)


def _lane_half(shape):
    return lax.broadcasted_iota(jnp.int32, shape, len(shape) - 1) >> _BLK_SHIFT


def _alibi_cols(c0, c1, c2, c3, shape):
    lane = lax.broadcasted_iota(jnp.int32, shape, 1)
    z = jnp.zeros(shape, F32)
    return jnp.where(lane == 0, c0, jnp.where(lane == 1, c1, jnp.where(lane == 2, c2, jnp.where(lane == 3, c3, z))))


def _q_alibi(tq, t0, slope):
    t = t0 + lax.broadcasted_iota(jnp.int32, (tq, LANES), 0)
    hi = (t >> _BLK_SHIFT).astype(F32)
    lo = (t & _BLK_MASK).astype(F32)
    return _alibi_cols(slope * SEL_BLOCK * hi, slope * lo, slope * SEL_BLOCK, slope, (tq, LANES)).astype(MXU_DTYPE)


def _k_alibi(pos):
    hi = (pos >> _BLK_SHIFT).astype(F32)
    lo = (pos & _BLK_MASK).astype(F32)
    return _alibi_cols(-1.0, -1.0, hi, lo, pos.shape).astype(MXU_DTYPE)


def _build_qaug(qaug_scr, slab_of, t0, tq, other):
    half = _lane_half((tq, LANES))
    for r in range(N_REP):
        slab = slab_of(r)
        for g in range(N_GROUPS):
            fill = jnp.zeros_like(slab) if other is None else other
            qaug_scr[g, r * tq:(r + 1) * tq, 0:LANES] = jnp.where(half == g, slab, fill)
            qaug_scr[g, r * tq:(r + 1) * tq, LANES:AUG_K] = _q_alibi(tq, t0, float(_SLOPES[g, r]))


def _store_slabs(o_ref, per_group, tq):
    lane = lax.broadcasted_iota(jnp.int32, (tq, LANES), 1)
    for r in range(N_REP):
        a = per_group[0][r * tq:(r + 1) * tq]
        b = per_group[1][r * tq:(r + 1) * tq]
        o_ref[0, :, r * LANES:(r + 1) * LANES] = jnp.where(lane < HEAD_DIM, a, b).astype(o_ref.dtype)


def _cmp_kernel(q_ref, kc_ref, vc_ref, ovl_ref, o_ref, ns_ref, need_ref, qaug_scr, score_scr, cnt_scr, *, tq, n_cmp, n_sel):
    i = pl.program_id(1)
    t0 = i * tq
    _build_qaug(qaug_scr, lambda r: q_ref[0, :, r * LANES:(r + 1) * LANES], t0, tq, None)

    kc = kc_ref[0]
    vc = vc_ref[0]
    half_k = _lane_half((n_cmp, LANES))
    c_pos = lax.broadcasted_iota(jnp.int32, (n_cmp, LANES), 0) * CMP_STRIDE + (CMP_BLOCK - 1)
    k_ali = _k_alibi(c_pos)
    t_row = t0 + lax.broadcasted_iota(jnp.int32, (tq, n_cmp), 0)
    c_end = lax.broadcasted_iota(jnp.int32, (tq, n_cmp), 1) * CMP_STRIDE + (CMP_BLOCK - 1)
    vis = (c_end <= t_row)[None]

    outs, psl_t = [], []
    for g in range(N_GROUPS):
        kaug = jnp.concatenate([jnp.where(half_k == g, kc, jnp.zeros_like(kc)), k_ali], axis=1)
        s = _dot_nt(qaug_scr[g], kaug).reshape(N_REP, tq, n_cmp)
        s = jnp.where(vis, s, NEG_INF)
        m = jnp.max(s, axis=-1, keepdims=True)
        e = jnp.where(vis, jnp.exp(s - m), 0.0)
        l = jnp.sum(e, axis=-1, keepdims=True)
        p = e * jnp.where(l > 0.0, 1.0 / l, 0.0)
        pb = p.astype(MXU_DTYPE).reshape(N_REP * tq, n_cmp)
        outs.append(_dot(pb, vc))
        pt = _dot_nt(ovl_ref[...], pb)
        psl_t.append(pt[:, 0:tq] + pt[:, tq:2 * tq] + pt[:, 2 * tq:3 * tq] + pt[:, 3 * tq:4 * tq])
    _store_slabs(o_ref, outs, tq)

    blk = lax.broadcasted_iota(jnp.int32, (SEL_BLOCK, tq), 0)
    cur = (t0 + lax.broadcasted_iota(jnp.int32, (SEL_BLOCK, tq), 1)) >> _BLK_SHIFT
    valid = blk <= cur
    forced = (blk == 0) | (blk == cur) | (blk == cur - 1)
    for n, g in enumerate((1, 0)):
        score_scr[n] = jnp.where(forced, FORCE_SCORE, jnp.where(valid, psl_t[g][0:SEL_BLOCK], NEG_INF))
    cnt_scr[...] = jnp.zeros_like(cnt_scr)

    n_valid = jnp.minimum(((t0 + tq - 1) >> _BLK_SHIFT) + 1, n_sel)
    chunk = 8
    for c0 in range(0, n_sel, chunk):
        @pl.when((n_valid > SEL_TOPK) & (c0 < n_valid))
        def _(c0=c0):
            for n in range(N_GROUPS):
                score = score_scr[n]
                cnt = cnt_scr[n]
                for j in range(c0, min(c0 + chunk, n_sel)):
                    xj = jnp.broadcast_to(score[j:j + 1, :], (SEL_BLOCK, tq))
                    beats = (xj > score) | ((xj == score) & (blk > j))
                    cnt = cnt + jnp.where(beats, 1, 0)
                cnt_scr[n] = cnt

    notsel = [jnp.where((cnt_scr[n] < SEL_TOPK) & valid, 0.0, 1.0) for n in range(N_GROUPS)]
    ns_t = jnp.transpose(jnp.concatenate(notsel, axis=0))
    ns_ref[0] = ns_t.astype(ns_ref.dtype)
    need_ref[0, 0] = (1.0 - jnp.min(ns_t, axis=0, keepdims=True)).astype(jnp.int32)


def _overlap_t(t):
    nc_pad = t // CMP_STRIDE
    ns = t // SEL_BLOCK
    c_start = np.arange(nc_pad) * CMP_STRIDE
    s_start = np.arange(ns) * SEL_BLOCK
    ov = np.clip(np.minimum(c_start[:, None] + CMP_BLOCK, s_start[None, :] + SEL_BLOCK)
                 - np.maximum(c_start[:, None], s_start[None, :]), 0, None) / CMP_BLOCK
    ov[nc_pad - 1] = 0.0
    out = np.zeros((LANES, nc_pad), np.float32)
    out[0:ns] = ov.T
    out[HEAD_DIM:HEAD_DIM + ns] = ov.T
    return out


def _cmp_attention(p16, kc, vc, *, tq=256):
    b, t, _ = p16.shape
    tq = min(tq, t)
    n_cmp = t // CMP_STRIDE
    n_sel = t // SEL_BLOCK
    assert n_sel <= SEL_BLOCK and n_cmp % LANES == 0
    ovl = jnp.asarray(_overlap_t(t), MXU_DTYPE)
    return pl.pallas_call(
        functools.partial(_cmp_kernel, tq=tq, n_cmp=n_cmp, n_sel=n_sel),
        grid=(b, t // tq),
        in_specs=[
            pl.BlockSpec((1, tq, 512), lambda bi, i: (bi, i, 0)),
            pl.BlockSpec((1, n_cmp, LANES), lambda bi, i: (bi, 0, 0)),
            pl.BlockSpec((1, n_cmp, LANES), lambda bi, i: (bi, 0, 0)),
            pl.BlockSpec((LANES, n_cmp), lambda bi, i: (0, 0)),
        ],
        out_specs=[
            pl.BlockSpec((1, tq, 512), lambda bi, i: (bi, i, 0)),
            pl.BlockSpec((1, tq, LANES), lambda bi, i: (bi, i, 0)),
            pl.BlockSpec((1, 1, 1, LANES), lambda bi, i: (bi, i, 0, 0)),
        ],
        out_shape=[
            jax.ShapeDtypeStruct((b, t, 512), BRANCH_DTYPE),
            jax.ShapeDtypeStruct((b, t, LANES), MXU_DTYPE),
            jax.ShapeDtypeStruct((b, t // tq, 1, LANES), jnp.int32),
        ],
        scratch_shapes=[
            pltpu.VMEM((N_GROUPS, N_REP * tq, AUG_K), MXU_DTYPE),
            pltpu.VMEM((N_GROUPS, SEL_BLOCK, tq), F32),
            pltpu.VMEM((N_GROUPS, SEL_BLOCK, tq), jnp.int32),
        ],
        compiler_params=_params("arbitrary", "arbitrary"),
        name="cmp_attn",
    )(p16, kc, vc, ovl)


def _build_kv(k_ref, v_ref, kaug_scr, vaug_scr, *, tile, seq, pad_tiles, select):
    half = _lane_half((tile, LANES))
    lane = lax.broadcasted_iota(jnp.int32, (tile, LANES), 1)
    for c in range(pad_tiles):
        for g in range(N_GROUPS):
            pad = jnp.where((half != g) & ((lane & _BLK_MASK) == 0), -MASK_BIG, 0.0).astype(MXU_DTYPE)
            kaug_scr[g, c * tile:(c + 1) * tile, 0:LANES] = pad
            kaug_scr[g, c * tile:(c + 1) * tile, LANES:AUG_K] = jnp.zeros((tile, LANES), MXU_DTYPE)
            vaug_scr[g, c] = jnp.zeros((LANES, tile), MXU_DTYPE)

    def body(c, carry):
        r0 = pl.multiple_of(c * tile, tile)
        k = k_ref[0, pl.ds(r0, tile), :]
        v = v_ref[0, pl.ds(r0, tile), :].astype(F32)
        pos = r0 + lax.broadcasted_iota(jnp.int32, (tile, LANES), 0)
        k_ali = _k_alibi(pos)
        if select:
            off = jnp.where((lane & _BLK_MASK) == (pos >> _BLK_SHIFT), -MASK_BIG, 0.0).astype(MXU_DTYPE)
        else:
            off = jnp.zeros((tile, LANES), MXU_DTYPE)
        dst = pl.multiple_of(r0 + pad_tiles * tile, tile)
        for g in range(N_GROUPS):
            kaug_scr[g, pl.ds(dst, tile), 0:LANES] = jnp.where(half == g, k, off)
            kaug_scr[g, pl.ds(dst, tile), LANES:AUG_K] = k_ali
            vaug_scr[g, c + pad_tiles] = jnp.transpose(jnp.where(half == g, v, 1.0)).astype(MXU_DTYPE)
        return carry
    lax.fori_loop(0, seq // tile, body, 0)


def _mask_bias(tile, off, window):
    rows = N_REP * tile
    key = lax.broadcasted_iota(jnp.int32, (tile, rows), 0)
    qry = lax.broadcasted_iota(jnp.int32, (tile, rows), 1) & (tile - 1)
    dist = qry - key + off
    keep = dist >= 0
    if window is not None:
        keep = keep & (dist < window)
    return jnp.where(keep, 0.0, NEG_INF)


def _emit_slabs(o_ref, row0, per_group, tile):
    sub = lax.broadcasted_iota(jnp.int32, (LANES, N_REP * tile), 0)
    both = jnp.where(sub < HEAD_DIM, per_group[0], per_group[1])
    for r in range(N_REP):
        o_ref[0, row0:row0 + tile, r * LANES:(r + 1) * LANES] = jnp.transpose(
            both[:, r * tile:(r + 1) * tile]).astype(o_ref.dtype)


def _sel_kernel(need_ref, q_ref, ns_ref, k_ref, v_ref, o_ref, kaug_scr, vaug_scr, qaug_scr, s_a, s_b, bias_scr, acc_scr,
                m_scr, todo_scr, *, tile, seq):
    i = pl.program_id(1)
    t0 = i * tile
    n_q = seq // tile

    base = (pl.program_id(0) * n_q + i) * n_q

    def scan(j, n):
        wanted = need_ref[base + j] > 0

        @pl.when(wanted)
        def _():
            todo_scr[n] = j
        return n + wanted.astype(jnp.int32)
    n_todo = lax.fori_loop(0, i, scan, 0)
    todo_scr[n_todo] = i

    @pl.when(i == 0)
    def _():
        _build_kv(k_ref, v_ref, kaug_scr, vaug_scr, tile=tile, seq=seq, pad_tiles=0, select=True)
        bias_scr[...] = _mask_bias(tile, 0, None)

    _build_qaug(qaug_scr, lambda r: q_ref[0, :, r * LANES:(r + 1) * LANES], t0, tile, ns_ref[0])
    acc_scr[...] = jnp.zeros_like(acc_scr)
    m_scr[...] = jnp.full_like(m_scr, -3e38)

    def scores_into(s_buf, j):
        k0 = pl.multiple_of(j * tile, tile)
        for g in range(N_GROUPS):
            s_buf[g] = _dot_nt(kaug_scr[g, pl.ds(k0, tile), :], qaug_scr[g])

    def consume(s_buf, j, diagonal):
        for g in range(N_GROUPS):
            s = s_buf[g]
            if diagonal:
                s = s + bias_scr[...]
            m_old = m_scr[g]
            m_new = jnp.maximum(m_old, jnp.max(s, axis=0, keepdims=True))
            p = jnp.exp(s - m_new).astype(MXU_DTYPE)
            acc_scr[g] = acc_scr[g] * jnp.exp(m_old - m_new) + _dot(vaug_scr[g, j], p)
            m_scr[g] = m_new

    scores_into(s_a, todo_scr[0])

    def pair(jj, carry):
        first, second = todo_scr[2 * jj], todo_scr[2 * jj + 1]
        scores_into(s_b, second)
        consume(s_a, first, False)
        scores_into(s_a, todo_scr[2 * jj + 2])
        consume(s_b, second, False)
        return carry
    lax.fori_loop(0, n_todo >> 1, pair, 0)

    @pl.when((n_todo & 1) == 0)
    def _():
        consume(s_a, i, True)

    @pl.when((n_todo & 1) == 1)
    def _():
        scores_into(s_b, i)
        consume(s_a, todo_scr[n_todo - 1], False)
        consume(s_b, i, True)

    outs = []
    for g in range(N_GROUPS):
        acc = acc_scr[g]
        den_row = (1 - g) * HEAD_DIM
        outs.append(acc / acc[den_row:den_row + 1, :])
    _emit_slabs(o_ref, 0, outs, tile)


def _sel_attention(p16, ns, block_need, q_blk, k_blk, v_blk, *, tile):
    b, t, _ = p16.shape
    tile = min(tile, t)
    n_q = t // tile
    rows = N_REP * tile
    assert block_need.shape[1] == n_q
    per_block = jnp.maximum(block_need[:, :, 0, 0:SEL_BLOCK], block_need[:, :, 0, SEL_BLOCK:])[:, :, 0:t // SEL_BLOCK]
    tile_need = per_block.reshape(b, n_q, n_q, tile // SEL_BLOCK).max(axis=-1).reshape(-1)
    grid_spec = pltpu.PrefetchScalarGridSpec(
        num_scalar_prefetch=1,
        grid=(b, n_q),
        in_specs=[
            pl.BlockSpec((1, tile, 512), lambda bi, i, need: (bi, i, q_blk)),
            pl.BlockSpec((1, tile, LANES), lambda bi, i, need: (bi, i, 0)),
            pl.BlockSpec((1, t, LANES), lambda bi, i, need: (bi, 0, k_blk)),
            pl.BlockSpec((1, t, LANES), lambda bi, i, need: (bi, 0, v_blk)),
        ],
        out_specs=pl.BlockSpec((1, tile, 512), lambda bi, i, need: (bi, i, 0)),
        scratch_shapes=[
            pltpu.VMEM((N_GROUPS, t, AUG_K), MXU_DTYPE),
            pltpu.VMEM((N_GROUPS, t // tile, LANES, tile), MXU_DTYPE),
            pltpu.VMEM((N_GROUPS, rows, AUG_K), MXU_DTYPE),
            pltpu.VMEM((N_GROUPS, tile, rows), F32),
            pltpu.VMEM((N_GROUPS, tile, rows), F32),
            pltpu.VMEM((tile, rows), F32),
            pltpu.VMEM((N_GROUPS, LANES, rows), F32),
            pltpu.VMEM((N_GROUPS, 1, rows), F32),
            pltpu.SMEM((n_q + 1,), jnp.int32),
        ],
    )
    return pl.pallas_call(
        functools.partial(_sel_kernel, tile=tile, seq=t),
        grid_spec=grid_spec,
        out_shape=jax.ShapeDtypeStruct((b, t, 512), BRANCH_DTYPE),
        compiler_params=_params("arbitrary", "arbitrary"),
        name="attn_sel",
    )(tile_need, p16, ns, p16, p16)


def _band_kernel(*refs, tile, seq, n_prev, window, has_sink):
    refs = list(refs)
    sink_ref = refs.pop(0) if has_sink else None
    q_ref, qn_ref, k_ref, v_ref, o_ref, kaug_scr, vaug_scr, qaug_a, qaug_b, s_a, s_b, bias_scr = refs
    i2 = pl.program_id(1)
    n_q = seq // tile
    span = n_prev + 1
    rows = N_REP * tile
    lane = lax.broadcasted_iota(jnp.int32, (tile, LANES), 1)
    pad_flag = jnp.where((lane & _BLK_MASK) == 0, 1.0, 0.0).astype(MXU_DTYPE)
    offs = [(n_prev - n) * tile for n in range(span)]
    masked = [n for n in range(span) if offs[n] - (tile - 1) < 0 or offs[n] + tile - 1 >= window]

    def scores_into(s_buf, qaug_buf, qt):
        k0 = pl.multiple_of(qt * tile, tile)
        for g in range(N_GROUPS):
            s = _dot_nt(kaug_scr[g, pl.ds(k0, span * tile), :], qaug_buf[g])
            for n in range(span):
                part = s[n * tile:(n + 1) * tile]
                s_buf[g, n * tile:(n + 1) * tile, :] = part + bias_scr[masked.index(n)] if n in masked else part

    def finish(s_buf, qt, row0):
        outs = []
        for g in range(N_GROUPS):
            def piece(n):
                return s_buf[g, n * tile:(n + 1) * tile, :]
            m = jnp.max(piece(0), axis=0, keepdims=True)
            for n in range(1, span):
                m = jnp.maximum(m, jnp.max(piece(n), axis=0, keepdims=True))
            pv = None
            for n in range(span):
                p = jnp.exp(piece(n) - m).astype(MXU_DTYPE)
                d = _dot(vaug_scr[g, qt + n], p)
                pv = d if pv is None else pv + d
            den_row = (1 - g) * HEAD_DIM
            den = pv[den_row:den_row + 1, :]
            if has_sink:
                col = lax.broadcasted_iota(jnp.int32, (1, rows), 1)
                sink = jnp.zeros((1, rows), F32)
                for r in range(N_REP):
                    sink = jnp.where(col >= r * tile, sink_ref[g * N_REP + r], sink)
                m2 = jnp.maximum(m, sink)
                scale = jnp.exp(m - m2)
                pv = pv * scale
                den = den * scale + jnp.exp(sink - m2)
            outs.append(pv / den)
        _emit_slabs(o_ref, row0, outs, tile)

    @pl.when(i2 == 0)
    def _():
        _build_kv(k_ref, v_ref, kaug_scr, vaug_scr, tile=tile, seq=seq, pad_tiles=n_prev, select=False)
        for idx, n in enumerate(masked):
            bias_scr[idx] = _mask_bias(tile, offs[n], window)
        _build_qaug(qaug_a, lambda r: q_ref[0, 0:tile, r * LANES:(r + 1) * LANES], 0, tile, pad_flag)
        scores_into(s_a, qaug_a, 0)

    qt = 2 * i2
    _build_qaug(qaug_b, lambda r: q_ref[0, tile:2 * tile, r * LANES:(r + 1) * LANES], (qt + 1) * tile, tile, pad_flag)
    scores_into(s_b, qaug_b, qt + 1)
    finish(s_a, qt, 0)
    qn = jnp.minimum(qt + 2, n_q - 1)
    _build_qaug(qaug_a, lambda r: qn_ref[0, :, r * LANES:(r + 1) * LANES], qn * tile, tile, pad_flag)
    scores_into(s_a, qaug_a, qn)
    finish(s_b, qt + 1, tile)


def _band_attention(p16, q_blk, k_blk, v_blk, *, tile, window, sinks=None):
    b, t, _ = p16.shape
    tile = min(tile, t // 2)
    n_q = t // tile
    n_prev = -(-(window - 1) // tile)
    span = n_prev + 1
    assert n_q % 2 == 0
    has_sink = sinks is not None
    rows = N_REP * tile
    offs = [(n_prev - n) * tile for n in range(span)]
    n_masked = sum(1 for o in offs if o - (tile - 1) < 0 or o + tile - 1 >= window)
    in_specs, args = [], []
    if has_sink:
        in_specs.append(pl.BlockSpec(memory_space=pltpu.SMEM))
        args.append(sinks.astype(F32))
    in_specs += [
        pl.BlockSpec((1, 2 * tile, 512), lambda bi, i: (bi, i, q_blk)),
        pl.BlockSpec((1, tile, 512), lambda bi, i: (bi, jnp.minimum(2 * i + 2, n_q - 1), q_blk)),
        pl.BlockSpec((1, t, LANES), lambda bi, i: (bi, 0, k_blk)),
        pl.BlockSpec((1, t, LANES), lambda bi, i: (bi, 0, v_blk)),
    ]
    args += [p16, p16, p16, p16]
    return pl.pallas_call(
        functools.partial(_band_kernel, tile=tile, seq=t, n_prev=n_prev, window=window, has_sink=has_sink),
        grid=(b, n_q // 2),
        in_specs=in_specs,
        out_specs=pl.BlockSpec((1, 2 * tile, 512), lambda bi, i: (bi, i, 0)),
        out_shape=jax.ShapeDtypeStruct((b, t, 512), BRANCH_DTYPE),
        scratch_shapes=[
            pltpu.VMEM((N_GROUPS, t + n_prev * tile, AUG_K), MXU_DTYPE),
            pltpu.VMEM((N_GROUPS, n_q + n_prev, LANES, tile), MXU_DTYPE),
            pltpu.VMEM((N_GROUPS, rows, AUG_K), MXU_DTYPE),
            pltpu.VMEM((N_GROUPS, rows, AUG_K), MXU_DTYPE),
            pltpu.VMEM((N_GROUPS, span * tile, rows), F32),
            pltpu.VMEM((N_GROUPS, span * tile, rows), F32),
            pltpu.VMEM((n_masked, tile, rows), F32),
        ],
        compiler_params=_params("arbitrary", "arbitrary"),
        name="attn_band" + ("_sink" if has_sink else ""),
    )(*args)


def _slab_cols(w):
    lead = w.shape[:-1]
    return w.reshape(*lead, N_GROUPS, N_REP, HEAD_DIM).swapaxes(-3, -2).reshape(*lead, N_GROUPS * N_REP * HEAD_DIM)


def _arrange_proj_weight(w):
    q_a = _slab_cols(w[:, 0:512])
    kc, vc, ks, vs, kw, vw = [w[:, 512 + j * LANES:512 + (j + 1) * LANES] for j in range(6)]
    gates = w[:, 1280:1304].reshape(-1, N_GROUPS, N_REP, N_BRANCH).transpose(0, 3, 2, 1).reshape(-1, 24)
    gates = jnp.pad(gates, ((0, 0), (0, LANES - 24)))
    q_b = _slab_cols(w[:, 1304:1816])
    k_b, v_b = w[:, 1816:1944], w[:, 1944:2072]
    return jnp.concatenate([q_a, q_b, ks, vs, kw, vw, k_b, v_b, kc, vc, gates], axis=1).astype(MXU_DTYPE)


def _arrange_out_weight(w):
    w_a = _slab_cols(w[0:512].T).T
    w_b = _slab_cols(w[512:1024].T).T
    return jnp.concatenate([w_a, w_b], axis=0).astype(MXU_DTYPE)


_QA_BLK, _QB_BLK = 0, 1
_KS_BLK, _VS_BLK, _KW_BLK, _VW_BLK, _KB_BLK, _VB_BLK = 8, 9, 10, 11, 12, 13


def kernel(x, ffn1_norm, ffn1_w_in, ffn1_w_out, mix_norm, w_mix_in, cmp_k_pos, cmp_k_w1, cmp_k_b1, cmp_k_w2, cmp_v_pos, cmp_v_w1, cmp_v_b1, cmp_v_w2, swa_sinks, w_mix_out, ffn2_norm, ffn2_w_in, ffn2_w_out, final_norm):
    b, t, d = x.shape
    m = b * t
    depth = ffn1_norm.shape[0]
    h = x.reshape(m, d)
    for l in range(depth):
        last = l == depth - 1
        h, p16, kc_raw, vc_raw, gates = _ffn_proj(
            h, ffn1_norm[l], ffn1_w_in[l].astype(MXU_DTYPE), ffn1_w_out[l].astype(MXU_DTYPE),
            mix_norm[l], _arrange_proj_weight(w_mix_in[l]))
        p16 = p16.reshape(b, t, P16_W)
        kc = _compress(kc_raw.reshape(b, t, LANES), cmp_k_pos[l], cmp_k_w1[l], cmp_k_b1[l], cmp_k_w2[l])
        vc = _compress(vc_raw.reshape(b, t, LANES), cmp_v_pos[l], cmp_v_w1[l], cmp_v_b1[l], cmp_v_w2[l])
        o_cmp, notsel, block_need = _cmp_attention(p16, kc, vc)
        o_slc = _sel_attention(p16, notsel, block_need, _QA_BLK, _KS_BLK, _VS_BLK, tile=256)
        o_win = _band_attention(p16, _QA_BLK, _KW_BLK, _VW_BLK, tile=256, window=NSA_WINDOW)
        o_b = _band_attention(p16, _QB_BLK, _KB_BLK, _VB_BLK, tile=128, window=SWA_WINDOW, sinks=swa_sinks[l])
        h = _combine_ffn(
            h, o_cmp.reshape(m, 512), o_slc.reshape(m, 512), o_win.reshape(m, 512), o_b.reshape(m, 512), gates,
            _arrange_out_weight(w_mix_out[l]), ffn2_norm[l], ffn2_w_in[l].astype(MXU_DTYPE),
            ffn2_w_out[l].astype(MXU_DTYPE), final_norm if last else None)
    if depth == 0:
        raise ValueError("depth must be positive")
    return h.reshape(b, t, d)
```

```python
import functools

import numpy as np
import jax
import jax.numpy as jnp
from jax import lax
from jax.experimental import pallas as pl
from jax.experimental.pallas import tpu as pltpu

F32 = jnp.float32
MXU_DTYPE = jnp.bfloat16
BRANCH_DTYPE = jnp.bfloat16

D_MODEL = 1024
HEAD_DIM = 64
N_GROUPS = 2
N_REP = 4
CMP_BLOCK = 32
CMP_STRIDE = 16
CMP_HIDDEN = 256
SEL_BLOCK = 64
SEL_TOPK = 16
NSA_WINDOW = 512
SWA_WINDOW = 128
SEL_TILE = 256
D_FF = 2816
NORM_EPS = 1e-6
NEG_INF = -1e30
FORCE_SCORE = 1e9
ATTN_SCALE = HEAD_DIM ** -0.5
MASK_BIG = 2.0 ** 100
LANES = 128
AUG_K = 2 * LANES
_BLK_SHIFT = 6
_BLK_MASK = SEL_BLOCK - 1
VMEM_LIMIT = 48 * 1024 * 1024

_SLOPES = (2.0 ** (-8.0 * np.arange(1, 9) / 8)).reshape(N_GROUPS, N_REP)

_NT = (((1,), (1,)), ((), ()))


def _dot(a, b):
    return jnp.dot(a, b, preferred_element_type=F32)


def _dot_nt(a, b):
    return lax.dot_general(a, b, _NT, preferred_element_type=F32)


def _rmsnorm(x, g):
    ms = jnp.mean(x * x, axis=-1, keepdims=True)
    return x * lax.rsqrt(ms + NORM_EPS) * g


def _params(*sem):
    return pltpu.CompilerParams(dimension_semantics=sem, vmem_limit_bytes=VMEM_LIMIT)


FFN_CHUNK = 256
FFN_ROWS = 512


def _swiglu_half_step(x, g_ref, wi_ref, wo_ref):
    xn = _rmsnorm(x, g_ref[...]).astype(MXU_DTYPE)
    acc = None
    for f in range(D_FF // FFN_CHUNK):
        cols = slice(f * FFN_CHUNK, (f + 1) * FFN_CHUNK)
        gate = _dot(xn, wi_ref[:, cols])
        up = _dot(xn, wi_ref[:, D_FF + f * FFN_CHUNK:D_FF + (f + 1) * FFN_CHUNK])
        act = gate * jax.nn.sigmoid(gate) * up
        part = _dot(act.astype(MXU_DTYPE), wo_ref[cols, :])
        acc = part if acc is None else acc + part
    return x + 0.5 * acc


P16_W = 1024 + 6 * LANES
PROJ_W = P16_W + 3 * LANES


def _ffn_proj_kernel(x_ref, g_ref, wi_ref, wo_ref, pg_ref, pw_ref, h_ref, p16_ref, kc_ref, vc_ref, gt_ref):
    h = _swiglu_half_step(x_ref[...], g_ref, wi_ref, wo_ref)
    h_ref[...] = h
    y = _dot(_rmsnorm(h, pg_ref[...]).astype(MXU_DTYPE), pw_ref[...])
    p16_ref[:, 0:1024] = (y[:, 0:1024] * ATTN_SCALE).astype(MXU_DTYPE)
    p16_ref[:, 1024:P16_W] = y[:, 1024:P16_W].astype(MXU_DTYPE)
    kc_ref[...] = y[:, P16_W:P16_W + LANES]
    vc_ref[...] = y[:, P16_W + LANES:P16_W + 2 * LANES]
    gt_ref[...] = y[:, P16_W + 2 * LANES:PROJ_W]


def _resident(shape):
    return pl.BlockSpec(shape, lambda i: (0,) * len(shape), pipeline_mode=pl.Buffered(1))


def _ffn_proj(x2, norm_g, w_in, w_out, proj_g, proj_w):
    m = x2.shape[0]
    tm = min(FFN_ROWS, m)
    row = lambda i: (i, 0)
    return pl.pallas_call(
        _ffn_proj_kernel,
        grid=(m // tm,),
        in_specs=[
            pl.BlockSpec((tm, D_MODEL), row),
            _resident((1, D_MODEL)),
            _resident((D_MODEL, 2 * D_FF)),
            _resident((D_FF, D_MODEL)),
            _resident((1, D_MODEL)),
            _resident((D_MODEL, PROJ_W)),
        ],
        out_specs=[
            pl.BlockSpec((tm, D_MODEL), row),
            pl.BlockSpec((tm, P16_W), row),
            pl.BlockSpec((tm, LANES), row),
            pl.BlockSpec((tm, LANES), row),
            pl.BlockSpec((tm, LANES), row),
        ],
        out_shape=[
            jax.ShapeDtypeStruct((m, D_MODEL), F32),
            jax.ShapeDtypeStruct((m, P16_W), MXU_DTYPE),
            jax.ShapeDtypeStruct((m, LANES), F32),
            jax.ShapeDtypeStruct((m, LANES), F32),
            jax.ShapeDtypeStruct((m, LANES), F32),
        ],
        compiler_params=_params("arbitrary"),
        name="ffn_proj",
    )(x2, norm_g.reshape(1, D_MODEL), w_in, w_out, proj_g.reshape(1, D_MODEL), proj_w)


N_BRANCH = 3
GATE_EXP_W = N_BRANCH * N_REP * LANES


def _combine_ffn_kernel(h_ref, oc_ref, os_ref, ow_ref, ob_ref, gt_ref, e_ref, w_ref, g_ref, wi_ref, wo_ref, *rest,
                        final_norm):
    if final_norm:
        fg_ref, o_ref = rest
    else:
        (o_ref,) = rest
    sig = jax.nn.sigmoid(gt_ref[...])
    hi = sig.astype(MXU_DTYPE)
    lo = (sig - hi.astype(F32)).astype(MXU_DTYPE)
    gx = _dot(hi, e_ref[...]) + _dot(lo, e_ref[...])
    acc = h_ref[...]
    branches = (oc_ref, os_ref, ow_ref)
    for r in range(N_REP):
        sl = slice(r * LANES, (r + 1) * LANES)
        o_a = None
        for br in range(N_BRANCH):
            c = (br * N_REP + r) * LANES
            term = gx[:, c:c + LANES] * branches[br][:, sl]
            o_a = term if o_a is None else o_a + term
        acc = acc + _dot(o_a.astype(MXU_DTYPE), w_ref[sl, :])
    acc = acc + _dot(ob_ref[...].astype(MXU_DTYPE), w_ref[512:1024, :])
    h = _swiglu_half_step(acc, g_ref, wi_ref, wo_ref)
    if final_norm:
        h = _rmsnorm(h, fg_ref[...])
    o_ref[...] = h


def _gate_expand():
    e = np.zeros((LANES, GATE_EXP_W), np.float32)
    for br in range(N_BRANCH):
        for r in range(N_REP):
            for g in range(N_GROUPS):
                c0 = (br * N_REP + r) * LANES + g * HEAD_DIM
                e[br * 8 + r * 2 + g, c0:c0 + HEAD_DIM] = 1.0
    return e


def _combine_ffn(h2, o_cmp, o_slc, o_win, o_b, gates, w_out_p, norm_g, w_in, w_out, final_g=None):
    m = h2.shape[0]
    tm = min(FFN_ROWS, m)
    row = lambda i: (i, 0)
    final_norm = final_g is not None
    e = jnp.asarray(_gate_expand(), MXU_DTYPE)
    in_specs = [
        pl.BlockSpec((tm, D_MODEL), row),
        pl.BlockSpec((tm, 512), row),
        pl.BlockSpec((tm, 512), row),
        pl.BlockSpec((tm, 512), row),
        pl.BlockSpec((tm, 512), row),
        pl.BlockSpec((tm, LANES), row),
        _resident((LANES, GATE_EXP_W)),
        _resident((D_MODEL, D_MODEL)),
        _resident((1, D_MODEL)),
        _resident((D_MODEL, 2 * D_FF)),
        _resident((D_FF, D_MODEL)),
    ]
    args = [h2, o_cmp, o_slc, o_win, o_b, gates, e, w_out_p, norm_g.reshape(1, D_MODEL), w_in, w_out]
    if final_norm:
        in_specs.append(_resident((1, D_MODEL)))
        args.append(final_g.reshape(1, D_MODEL))
    return pl.pallas_call(
        functools.partial(_combine_ffn_kernel, final_norm=final_norm),
        grid=(m // tm,),
        in_specs=in_specs,
        out_specs=pl.BlockSpec((tm, D_MODEL), row),
        out_shape=jax.ShapeDtypeStruct((m, D_MODEL), F32),
        compiler_params=_params("arbitrary"),
        name="combine_ffn_final" if final_norm else "combine_ffn",
    )(*args)


def _gelu_tanh(x):
    return 0.5 * x * (1.0 + jnp.tanh(np.sqrt(2.0 / np.pi) * (x + 0.044715 * (x * x * x))))


def _compress_kernel(raw_ref, pa_ref, pb_ref, wa_ref, wb_ref, b1_ref, w2_ref, o_ref):
    x = raw_ref[0]
    n_rows = x.shape[0]
    first = _dot((x + pa_ref[...]).astype(MXU_DTYPE), wa_ref[...])
    second = _dot((x + pb_ref[...]).astype(MXU_DTYPE), wb_ref[...])
    hid = first + pltpu.roll(second, n_rows - 1, axis=0) + b1_ref[...]
    out = _dot(_gelu_tanh(hid).astype(MXU_DTYPE), w2_ref[...])
    row = lax.broadcasted_iota(jnp.int32, out.shape, 0)
    o_ref[0] = jnp.where(row == n_rows - 1, 0.0, out).astype(o_ref.dtype)


def _compress(raw, pos, w1, b1, w2):
    b, t, _ = raw.shape
    n_rows = t // CMP_STRIDE
    half = CMP_STRIDE
    kw = half * LANES
    eye = jnp.eye(N_GROUPS, dtype=F32)
    w1r = w1.reshape(CMP_BLOCK, HEAD_DIM, CMP_HIDDEN)

    def expand(wpart):
        return jnp.einsum("jdh,ab->jadbh", wpart, eye).reshape(kw, N_GROUPS * CMP_HIDDEN).astype(MXU_DTYPE)

    def expand_pos(ppart):
        return jnp.broadcast_to(ppart[:, None, :], (half, N_GROUPS, HEAD_DIM)).reshape(1, kw)

    wa, wb = expand(w1r[:half]), expand(w1r[half:])
    pa, pb = expand_pos(pos[:half]), expand_pos(pos[half:])
    b1e = jnp.tile(b1.reshape(1, CMP_HIDDEN), (1, N_GROUPS))
    w2e = jnp.einsum("hd,ab->ahbd", w2, eye).reshape(N_GROUPS * CMP_HIDDEN, LANES).astype(MXU_DTYPE)
    const = lambda i: (0, 0)
    return pl.pallas_call(
        _compress_kernel,
        grid=(b,),
        in_specs=[
            pl.BlockSpec((1, n_rows, kw), lambda i: (i, 0, 0)),
            pl.BlockSpec((1, kw), const),
            pl.BlockSpec((1, kw), const),
            pl.BlockSpec((kw, N_GROUPS * CMP_HIDDEN), const),
            pl.BlockSpec((kw, N_GROUPS * CMP_HIDDEN), const),
            pl.BlockSpec((1, N_GROUPS * CMP_HIDDEN), const),
            pl.BlockSpec((N_GROUPS * CMP_HIDDEN, LANES), const),
        ],
        out_specs=pl.BlockSpec((1, n_rows, LANES), lambda i: (i, 0, 0)),
        out_shape=jax.ShapeDtypeStruct((b, n_rows, LANES), MXU_DTYPE),
        compiler_params=_params("arbitrary"),
        name="compress",
    )(raw.reshape(b, n_rows, kw), pa, pb, wa, wb, b1e, w2e)


def _lane_half(shape):
    return lax.broadcasted_iota(jnp.int32, shape, len(shape) - 1) >> _BLK_SHIFT


def _alibi_cols(c0, c1, c2, c3, shape):
    lane = lax.broadcasted_iota(jnp.int32, shape, 1)
    z = jnp.zeros(shape, F32)
    return jnp.where(lane == 0, c0, jnp.where(lane == 1, c1, jnp.where(lane == 2, c2, jnp.where(lane == 3, c3, z))))


def _q_alibi(tq, t0, slope):
    t = t0 + lax.broadcasted_iota(jnp.int32, (tq, LANES), 0)
    hi = (t >> _BLK_SHIFT).astype(F32)
    lo = (t & _BLK_MASK).astype(F32)
    return _alibi_cols(slope * SEL_BLOCK * hi, slope * lo, slope * SEL_BLOCK, slope, (tq, LANES)).astype(MXU_DTYPE)


def _k_alibi(pos):
    hi = (pos >> _BLK_SHIFT).astype(F32)
    lo = (pos & _BLK_MASK).astype(F32)
    return _alibi_cols(-1.0, -1.0, hi, lo, pos.shape).astype(MXU_DTYPE)


def _build_qaug(qaug_scr, slab_of, t0, tq, other):
    half = _lane_half((tq, LANES))
    unit = _q_alibi(tq, t0, 1.0)
    for r in range(N_REP):
        slab = slab_of(r)
        for g in range(N_GROUPS):
            fill = jnp.zeros_like(slab) if other is None else other
            qaug_scr[g, r * tq:(r + 1) * tq, 0:LANES] = jnp.where(half == g, slab, fill)
            qaug_scr[g, r * tq:(r + 1) * tq, LANES:AUG_K] = unit * jnp.asarray(_SLOPES[g, r], MXU_DTYPE)


def _cmp_kernel(q_ref, kc_ref, vc_ref, ovl_ref, o_ref, ns_ref, need_ref, qaug_scr, score_scr, cnt_scr, *, tq, n_cmp, n_sel):
    i = pl.program_id(1)
    t0 = i * tq
    _build_qaug(qaug_scr, lambda r: q_ref[0, :, r * LANES:(r + 1) * LANES], t0, tq, None)

    kc = kc_ref[0]
    vc = vc_ref[0]
    half_k = _lane_half((n_cmp, LANES))
    c_pos = lax.broadcasted_iota(jnp.int32, (n_cmp, LANES), 0) * CMP_STRIDE + (CMP_BLOCK - 1)
    k_ali = _k_alibi(c_pos)
    rows = N_REP * tq
    c_end = lax.broadcasted_iota(jnp.int32, (n_cmp, rows), 0) * CMP_STRIDE + (CMP_BLOCK - 1)
    t_col = t0 + (lax.broadcasted_iota(jnp.int32, (n_cmp, rows), 1) & (tq - 1))
    vis = c_end <= t_col
    vc_t = jnp.transpose(vc.astype(F32)).astype(MXU_DTYPE)

    outs, psl_t = [], []
    for g in range(N_GROUPS):
        kaug = jnp.concatenate([jnp.where(half_k == g, kc, jnp.zeros_like(kc)), k_ali], axis=1)
        s = jnp.where(vis, _dot_nt(kaug, qaug_scr[g]), NEG_INF)
        m = jnp.max(s, axis=0, keepdims=True)
        e = jnp.where(vis, jnp.exp(s - m), 0.0)
        l = jnp.sum(e, axis=0, keepdims=True)
        pb = (e * jnp.where(l > 0.0, 1.0 / l, 0.0)).astype(MXU_DTYPE)
        outs.append(_dot(vc_t, pb))
        pt = _dot(ovl_ref[...], pb)
        psl_t.append(pt[:, 0:tq] + pt[:, tq:2 * tq] + pt[:, 2 * tq:3 * tq] + pt[:, 3 * tq:4 * tq])
    _emit_slabs(o_ref, 0, outs, tq)

    blk = lax.broadcasted_iota(jnp.int32, (SEL_BLOCK, tq), 0)
    cur = (t0 + lax.broadcasted_iota(jnp.int32, (SEL_BLOCK, tq), 1)) >> _BLK_SHIFT
    valid = blk <= cur
    forced = (blk == 0) | (blk == cur) | (blk == cur - 1)
    for n, g in enumerate((1, 0)):
        score_scr[n] = jnp.where(forced, FORCE_SCORE, jnp.where(valid, psl_t[g][0:SEL_BLOCK], NEG_INF))
    cnt_scr[...] = jnp.zeros_like(cnt_scr)

    n_valid = jnp.minimum(((t0 + tq - 1) >> _BLK_SHIFT) + 1, n_sel)
    chunk = 8
    for c0 in range(0, n_sel, chunk):
        @pl.when((n_valid > SEL_TOPK) & (c0 < n_valid))
        def _(c0=c0):
            for n in range(N_GROUPS):
                score = score_scr[n]
                cnt = cnt_scr[n]
                for j in range(c0, min(c0 + chunk, n_sel)):
                    xj = jnp.broadcast_to(score[j:j + 1, :], (SEL_BLOCK, tq))
                    beats = (xj > score) | ((xj == score) & (blk > j))
                    cnt = cnt + jnp.where(beats, 1, 0)
                cnt_scr[n] = cnt

    notsel = [jnp.where((cnt_scr[n] < SEL_TOPK) & valid, 0.0, 1.0) for n in range(N_GROUPS)]
    ns_t = jnp.transpose(jnp.concatenate(notsel, axis=0))
    ns_ref[0] = ns_t.astype(ns_ref.dtype)
    for u in range(tq // SEL_TILE):
        part = ns_t[u * SEL_TILE:(u + 1) * SEL_TILE]
        need_ref[0, u] = (1.0 - jnp.min(part, axis=0, keepdims=True)).astype(jnp.int32)


def _overlap_t(t):
    nc_pad = t // CMP_STRIDE
    ns = t // SEL_BLOCK
    c_start = np.arange(nc_pad) * CMP_STRIDE
    s_start = np.arange(ns) * SEL_BLOCK
    ov = np.clip(np.minimum(c_start[:, None] + CMP_BLOCK, s_start[None, :] + SEL_BLOCK)
                 - np.maximum(c_start[:, None], s_start[None, :]), 0, None) / CMP_BLOCK
    ov[nc_pad - 1] = 0.0
    out = np.zeros((LANES, nc_pad), np.float32)
    out[0:ns] = ov.T
    out[HEAD_DIM:HEAD_DIM + ns] = ov.T
    return out


def _cmp_attention(p16, kc, vc, *, tq=512):
    b, t, _ = p16.shape
    tq = min(tq, t)
    assert tq % SEL_TILE == 0
    n_cmp = t // CMP_STRIDE
    n_sel = t // SEL_BLOCK
    assert n_sel <= SEL_BLOCK and n_cmp % LANES == 0
    ovl = jnp.asarray(_overlap_t(t), MXU_DTYPE)
    return pl.pallas_call(
        functools.partial(_cmp_kernel, tq=tq, n_cmp=n_cmp, n_sel=n_sel),
        grid=(b, t // tq),
        in_specs=[
            pl.BlockSpec((1, tq, 512), lambda bi, i: (bi, i, 0)),
            pl.BlockSpec((1, n_cmp, LANES), lambda bi, i: (bi, 0, 0)),
            pl.BlockSpec((1, n_cmp, LANES), lambda bi, i: (bi, 0, 0)),
            pl.BlockSpec((LANES, n_cmp), lambda bi, i: (0, 0)),
        ],
        out_specs=[
            pl.BlockSpec((1, tq, 512), lambda bi, i: (bi, i, 0)),
            pl.BlockSpec((1, tq, LANES), lambda bi, i: (bi, i, 0)),
            pl.BlockSpec((1, tq // SEL_TILE, 1, LANES), lambda bi, i: (bi, i, 0, 0)),
        ],
        out_shape=[
            jax.ShapeDtypeStruct((b, t, 512), BRANCH_DTYPE),
            jax.ShapeDtypeStruct((b, t, LANES), MXU_DTYPE),
            jax.ShapeDtypeStruct((b, t // SEL_TILE, 1, LANES), jnp.int32),
        ],
        scratch_shapes=[
            pltpu.VMEM((N_GROUPS, N_REP * tq, AUG_K), MXU_DTYPE),
            pltpu.VMEM((N_GROUPS, SEL_BLOCK, tq), F32),
            pltpu.VMEM((N_GROUPS, SEL_BLOCK, tq), jnp.int32),
        ],
        compiler_params=_params("arbitrary", "arbitrary"),
        name="cmp_attn",
    )(p16, kc, vc, ovl)


def _build_kv(k_ref, v_ref, kaug_scr, vaug_scr, *, tile, seq, pad_tiles, select):
    half = _lane_half((tile, LANES))
    lane = lax.broadcasted_iota(jnp.int32, (tile, LANES), 1)
    for c in range(pad_tiles):
        for g in range(N_GROUPS):
            pad = jnp.where((half != g) & ((lane & _BLK_MASK) == 0), -MASK_BIG, 0.0).astype(MXU_DTYPE)
            kaug_scr[g, c * tile:(c + 1) * tile, 0:LANES] = pad
            kaug_scr[g, c * tile:(c + 1) * tile, LANES:AUG_K] = jnp.zeros((tile, LANES), MXU_DTYPE)
            vaug_scr[g, c] = jnp.zeros((LANES, tile), MXU_DTYPE)

    def body(c, carry):
        r0 = pl.multiple_of(c * tile, tile)
        k = k_ref[0, pl.ds(r0, tile), :]
        v = v_ref[0, pl.ds(r0, tile), :].astype(F32)
        pos = r0 + lax.broadcasted_iota(jnp.int32, (tile, LANES), 0)
        k_ali = _k_alibi(pos)
        if select:
            off = jnp.where((lane & _BLK_MASK) == (pos >> _BLK_SHIFT), -MASK_BIG, 0.0).astype(MXU_DTYPE)
        else:
            off = jnp.zeros((tile, LANES), MXU_DTYPE)
        dst = pl.multiple_of(r0 + pad_tiles * tile, tile)
        v_t = jnp.transpose(v)
        sub_half = lax.broadcasted_iota(jnp.int32, (LANES, tile), 0) >> _BLK_SHIFT
        for g in range(N_GROUPS):
            kaug_scr[g, pl.ds(dst, tile), 0:LANES] = jnp.where(half == g, k, off)
            kaug_scr[g, pl.ds(dst, tile), LANES:AUG_K] = k_ali
            vaug_scr[g, c + pad_tiles] = jnp.where(sub_half == g, v_t, 1.0).astype(MXU_DTYPE)
        return carry
    lax.fori_loop(0, seq // tile, body, 0)


def _mask_bias(tile, off, window):
    rows = N_REP * tile
    key = lax.broadcasted_iota(jnp.int32, (tile, rows), 0)
    qry = lax.broadcasted_iota(jnp.int32, (tile, rows), 1) & (tile - 1)
    dist = qry - key + off
    keep = dist >= 0
    if window is not None:
        keep = keep & (dist < window)
    return jnp.where(keep, 0.0, NEG_INF)


def _emit_slabs(o_ref, row0, per_group, tile):
    sub = lax.broadcasted_iota(jnp.int32, (LANES, N_REP * tile), 0)
    both = jnp.where(sub < HEAD_DIM, per_group[0], per_group[1])
    for r in range(N_REP):
        o_ref[0, row0:row0 + tile, r * LANES:(r + 1) * LANES] = jnp.transpose(
            both[:, r * tile:(r + 1) * tile]).astype(o_ref.dtype)


def _sel_kernel(need_ref, q_ref, ns_ref, k_ref, v_ref, o_ref, kaug_scr, vaug_scr, qaug_scr, s_a, s_b, bias_scr, acc_scr,
                m_scr, todo_scr, *, tile, seq):
    i = pl.program_id(1)
    t0 = i * tile
    n_q = seq // tile

    base = (pl.program_id(0) * n_q + i) * n_q

    def scan(j, n):
        wanted = need_ref[base + j] > 0

        @pl.when(wanted)
        def _():
            todo_scr[n] = j
        return n + wanted.astype(jnp.int32)
    n_todo = lax.fori_loop(0, i, scan, 0)
    todo_scr[n_todo] = i

    @pl.when(i == 0)
    def _():
        _build_kv(k_ref, v_ref, kaug_scr, vaug_scr, tile=tile, seq=seq, pad_tiles=0, select=True)
        bias_scr[...] = _mask_bias(tile, 0, None)

    _build_qaug(qaug_scr, lambda r: q_ref[0, :, r * LANES:(r + 1) * LANES], t0, tile, ns_ref[0])
    acc_scr[...] = jnp.zeros_like(acc_scr)
    m_scr[...] = jnp.full_like(m_scr, -3e38)

    def scores_into(s_buf, j):
        k0 = pl.multiple_of(j * tile, tile)
        for g in range(N_GROUPS):
            s_buf[g] = _dot_nt(kaug_scr[g, pl.ds(k0, tile), :], qaug_scr[g])

    def consume(s_buf, j, diagonal):
        for g in range(N_GROUPS):
            s = s_buf[g]
            if diagonal:
                s = s + bias_scr[...]
            m_old = m_scr[g]
            m_new = jnp.maximum(m_old, jnp.max(s, axis=0, keepdims=True))
            p = jnp.exp(s - m_new).astype(MXU_DTYPE)
            acc_scr[g] = acc_scr[g] * jnp.exp(m_old - m_new) + _dot(vaug_scr[g, j], p)
            m_scr[g] = m_new

    scores_into(s_a, todo_scr[0])

    def pair(jj, carry):
        first, second = todo_scr[2 * jj], todo_scr[2 * jj + 1]
        scores_into(s_b, second)
        consume(s_a, first, False)
        scores_into(s_a, todo_scr[2 * jj + 2])
        consume(s_b, second, False)
        return carry
    lax.fori_loop(0, n_todo >> 1, pair, 0)

    @pl.when((n_todo & 1) == 0)
    def _():
        consume(s_a, i, True)

    @pl.when((n_todo & 1) == 1)
    def _():
        scores_into(s_b, i)
        consume(s_a, todo_scr[n_todo - 1], False)
        consume(s_b, i, True)

    outs = []
    for g in range(N_GROUPS):
        acc = acc_scr[g]
        den_row = (1 - g) * HEAD_DIM
        outs.append(acc / acc[den_row:den_row + 1, :])
    _emit_slabs(o_ref, 0, outs, tile)


def _sel_attention(p16, ns, block_need, q_blk, k_blk, v_blk, *, tile):
    b, t, _ = p16.shape
    tile = min(tile, t)
    n_q = t // tile
    rows = N_REP * tile
    assert block_need.shape[1] == n_q
    per_block = jnp.maximum(block_need[:, :, 0, 0:SEL_BLOCK], block_need[:, :, 0, SEL_BLOCK:])[:, :, 0:t // SEL_BLOCK]
    tile_need = per_block.reshape(b, n_q, n_q, tile // SEL_BLOCK).max(axis=-1).reshape(-1)
    grid_spec = pltpu.PrefetchScalarGridSpec(
        num_scalar_prefetch=1,
        grid=(b, n_q),
        in_specs=[
            pl.BlockSpec((1, tile, 512), lambda bi, i, need: (bi, i, q_blk)),
            pl.BlockSpec((1, tile, LANES), lambda bi, i, need: (bi, i, 0)),
            pl.BlockSpec((1, t, LANES), lambda bi, i, need: (bi, 0, k_blk)),
            pl.BlockSpec((1, t, LANES), lambda bi, i, need: (bi, 0, v_blk)),
        ],
        out_specs=pl.BlockSpec((1, tile, 512), lambda bi, i, need: (bi, i, 0)),
        scratch_shapes=[
            pltpu.VMEM((N_GROUPS, t, AUG_K), MXU_DTYPE),
            pltpu.VMEM((N_GROUPS, t // tile, LANES, tile), MXU_DTYPE),
            pltpu.VMEM((N_GROUPS, rows, AUG_K), MXU_DTYPE),
            pltpu.VMEM((N_GROUPS, tile, rows), F32),
            pltpu.VMEM((N_GROUPS, tile, rows), F32),
            pltpu.VMEM((tile, rows), F32),
            pltpu.VMEM((N_GROUPS, LANES, rows), F32),
            pltpu.VMEM((N_GROUPS, 1, rows), F32),
            pltpu.SMEM((n_q + 1,), jnp.int32),
        ],
    )
    return pl.pallas_call(
        functools.partial(_sel_kernel, tile=tile, seq=t),
        grid_spec=grid_spec,
        out_shape=jax.ShapeDtypeStruct((b, t, 512), BRANCH_DTYPE),
        compiler_params=_params("arbitrary", "arbitrary"),
        name="attn_sel",
    )(tile_need, p16, ns, p16, p16)


def _band_kernel(*refs, tile, seq, n_prev, window, has_sink):
    refs = list(refs)
    sink_ref = refs.pop(0) if has_sink else None
    q_ref, qn_ref, k_ref, v_ref, o_ref, kaug_scr, vaug_scr, qaug_a, qaug_b, s_a, s_b, bias_scr = refs
    i2 = pl.program_id(1)
    n_q = seq // tile
    span = n_prev + 1
    rows = N_REP * tile
    lane = lax.broadcasted_iota(jnp.int32, (tile, LANES), 1)
    pad_flag = jnp.where((lane & _BLK_MASK) == 0, 1.0, 0.0).astype(MXU_DTYPE)
    offs = [(n_prev - n) * tile for n in range(span)]
    masked = [n for n in range(span) if offs[n] - (tile - 1) < 0 or offs[n] + tile - 1 >= window]

    def scores_into(s_buf, qaug_buf, qt):
        k0 = pl.multiple_of(qt * tile, tile)
        for g in range(N_GROUPS):
            s = _dot_nt(kaug_scr[g, pl.ds(k0, span * tile), :], qaug_buf[g])
            for n in range(span):
                part = s[n * tile:(n + 1) * tile]
                s_buf[g, n * tile:(n + 1) * tile, :] = part + bias_scr[masked.index(n)] if n in masked else part

    def finish(s_buf, qt, row0):
        outs = []
        for g in range(N_GROUPS):
            def piece(n):
                return s_buf[g, n * tile:(n + 1) * tile, :]
            m = jnp.max(piece(0), axis=0, keepdims=True)
            for n in range(1, span):
                m = jnp.maximum(m, jnp.max(piece(n), axis=0, keepdims=True))
            pv = None
            for n in range(span):
                p = jnp.exp(piece(n) - m).astype(MXU_DTYPE)
                d = _dot(vaug_scr[g, qt + n], p)
                pv = d if pv is None else pv + d
            den_row = (1 - g) * HEAD_DIM
            den = pv[den_row:den_row + 1, :]
            if has_sink:
                col = lax.broadcasted_iota(jnp.int32, (1, rows), 1)
                sink = jnp.zeros((1, rows), F32)
                for r in range(N_REP):
                    sink = jnp.where(col >= r * tile, sink_ref[g * N_REP + r], sink)
                m2 = jnp.maximum(m, sink)
                scale = jnp.exp(m - m2)
                pv = pv * scale
                den = den * scale + jnp.exp(sink - m2)
            outs.append(pv / den)
        _emit_slabs(o_ref, row0, outs, tile)

    @pl.when(i2 == 0)
    def _():
        _build_kv(k_ref, v_ref, kaug_scr, vaug_scr, tile=tile, seq=seq, pad_tiles=n_prev, select=False)
        for idx, n in enumerate(masked):
            bias_scr[idx] = _mask_bias(tile, offs[n], window)
        _build_qaug(qaug_a, lambda r: q_ref[0, 0:tile, r * LANES:(r + 1) * LANES], 0, tile, pad_flag)
        scores_into(s_a, qaug_a, 0)

    qt = 2 * i2
    _build_qaug(qaug_b, lambda r: q_ref[0, tile:2 * tile, r * LANES:(r + 1) * LANES], (qt + 1) * tile, tile, pad_flag)
    finish(s_a, qt, 0)
    scores_into(s_b, qaug_b, qt + 1)
    qn = jnp.minimum(qt + 2, n_q - 1)
    _build_qaug(qaug_a, lambda r: qn_ref[0, :, r * LANES:(r + 1) * LANES], qn * tile, tile, pad_flag)
    scores_into(s_a, qaug_a, qn)
    finish(s_b, qt + 1, tile)


def _band_attention(p16, q_blk, k_blk, v_blk, *, tile, window, sinks=None):
    b, t, _ = p16.shape
    tile = min(tile, t // 2)
    n_q = t // tile
    n_prev = -(-(window - 1) // tile)
    span = n_prev + 1
    assert n_q % 2 == 0
    has_sink = sinks is not None
    rows = N_REP * tile
    offs = [(n_prev - n) * tile for n in range(span)]
    n_masked = sum(1 for o in offs if o - (tile - 1) < 0 or o + tile - 1 >= window)
    in_specs, args = [], []
    if has_sink:
        in_specs.append(pl.BlockSpec(memory_space=pltpu.SMEM))
        args.append(sinks.astype(F32))
    in_specs += [
        pl.BlockSpec((1, 2 * tile, 512), lambda bi, i: (bi, i, q_blk)),
        pl.BlockSpec((1, tile, 512), lambda bi, i: (bi, jnp.minimum(2 * i + 2, n_q - 1), q_blk)),
        pl.BlockSpec((1, t, LANES), lambda bi, i: (bi, 0, k_blk)),
        pl.BlockSpec((1, t, LANES), lambda bi, i: (bi, 0, v_blk)),
    ]
    args += [p16, p16, p16, p16]
    return pl.pallas_call(
        functools.partial(_band_kernel, tile=tile, seq=t, n_prev=n_prev, window=window, has_sink=has_sink),
        grid=(b, n_q // 2),
        in_specs=in_specs,
        out_specs=pl.BlockSpec((1, 2 * tile, 512), lambda bi, i: (bi, i, 0)),
        out_shape=jax.ShapeDtypeStruct((b, t, 512), BRANCH_DTYPE),
        scratch_shapes=[
            pltpu.VMEM((N_GROUPS, t + n_prev * tile, AUG_K), MXU_DTYPE),
            pltpu.VMEM((N_GROUPS, n_q + n_prev, LANES, tile), MXU_DTYPE),
            pltpu.VMEM((N_GROUPS, rows, AUG_K), MXU_DTYPE),
            pltpu.VMEM((N_GROUPS, rows, AUG_K), MXU_DTYPE),
            pltpu.VMEM((N_GROUPS, span * tile, rows), F32),
            pltpu.VMEM((N_GROUPS, span * tile, rows), F32),
            pltpu.VMEM((n_masked, tile, rows), F32),
        ],
        compiler_params=_params("arbitrary", "arbitrary"),
        name="attn_band" + ("_sink" if has_sink else ""),
    )(*args)


def _slab_cols(w):
    lead = w.shape[:-1]
    return w.reshape(*lead, N_GROUPS, N_REP, HEAD_DIM).swapaxes(-3, -2).reshape(*lead, N_GROUPS * N_REP * HEAD_DIM)


def _arrange_proj_weight(w):
    q_a = _slab_cols(w[:, 0:512])
    kc, vc, ks, vs, kw, vw = [w[:, 512 + j * LANES:512 + (j + 1) * LANES] for j in range(6)]
    gates = w[:, 1280:1304].reshape(-1, N_GROUPS, N_REP, N_BRANCH).transpose(0, 3, 2, 1).reshape(-1, 24)
    gates = jnp.pad(gates, ((0, 0), (0, LANES - 24)))
    q_b = _slab_cols(w[:, 1304:1816])
    k_b, v_b = w[:, 1816:1944], w[:, 1944:2072]
    return jnp.concatenate([q_a, q_b, ks, vs, kw, vw, k_b, v_b, kc, vc, gates], axis=1).astype(MXU_DTYPE)


def _arrange_out_weight(w):
    w_a = _slab_cols(w[0:512].T).T
    w_b = _slab_cols(w[512:1024].T).T
    return jnp.concatenate([w_a, w_b], axis=0).astype(MXU_DTYPE)


_QA_BLK, _QB_BLK = 0, 1
_KS_BLK, _VS_BLK, _KW_BLK, _VW_BLK, _KB_BLK, _VB_BLK = 8, 9, 10, 11, 12, 13


def kernel(x, ffn1_norm, ffn1_w_in, ffn1_w_out, mix_norm, w_mix_in, cmp_k_pos, cmp_k_w1, cmp_k_b1, cmp_k_w2, cmp_v_pos, cmp_v_w1, cmp_v_b1, cmp_v_w2, swa_sinks, w_mix_out, ffn2_norm, ffn2_w_in, ffn2_w_out, final_norm):
    b, t, d = x.shape
    m = b * t
    depth = ffn1_norm.shape[0]
    h = x.reshape(m, d)
    for l in range(depth):
        last = l == depth - 1
        h, p16, kc_raw, vc_raw, gates = _ffn_proj(
            h, ffn1_norm[l], ffn1_w_in[l].astype(MXU_DTYPE), ffn1_w_out[l].astype(MXU_DTYPE),
            mix_norm[l], _arrange_proj_weight(w_mix_in[l]))
        p16 = p16.reshape(b, t, P16_W)
        kc = _compress(kc_raw.reshape(b, t, LANES), cmp_k_pos[l], cmp_k_w1[l], cmp_k_b1[l], cmp_k_w2[l])
        vc = _compress(vc_raw.reshape(b, t, LANES), cmp_v_pos[l], cmp_v_w1[l], cmp_v_b1[l], cmp_v_w2[l])
        o_cmp, notsel, block_need = _cmp_attention(p16, kc, vc)
        o_slc = _sel_attention(p16, notsel, block_need, _QA_BLK, _KS_BLK, _VS_BLK, tile=SEL_TILE)
        o_win = _band_attention(p16, _QA_BLK, _KW_BLK, _VW_BLK, tile=256, window=NSA_WINDOW)
        o_b = _band_attention(p16, _QB_BLK, _KB_BLK, _VB_BLK, tile=128, window=SWA_WINDOW, sinks=swa_sinks[l])
        h = _combine_ffn(
            h, o_cmp.reshape(m, 512), o_slc.reshape(m, 512), o_win.reshape(m, 512), o_b.reshape(m, 512), gates,
            _arrange_out_weight(w_mix_out[l]), ffn2_norm[l], ffn2_w_in[l].astype(MXU_DTYPE),
            ffn2_w_out[l].astype(MXU_DTYPE), final_norm if last else None)
    if depth == 0:
        raise ValueError("depth must be positive")
    return h.reshape(b, t, d)
```

```python
import functools

import numpy as np
import jax
import jax.numpy as jnp
from jax import lax
from jax.experimental import pallas as pl
from jax.experimental.pallas import tpu as pltpu

F32 = jnp.float32
MXU_DTYPE = jnp.bfloat16
BRANCH_DTYPE = jnp.bfloat16

D_MODEL = 1024
HEAD_DIM = 64
N_GROUPS = 2
N_REP = 4
CMP_BLOCK = 32
CMP_STRIDE = 16
CMP_HIDDEN = 256
SEL_BLOCK = 64
SEL_TOPK = 16
NSA_WINDOW = 512
SWA_WINDOW = 128
SEL_TILE = 256
D_FF = 2816
NORM_EPS = 1e-6
NEG_INF = -1e30
FORCE_SCORE = 1e9
ATTN_SCALE = HEAD_DIM ** -0.5
MASK_BIG = 2.0 ** 100
LANES = 128
AUG_K = 2 * LANES
_BLK_SHIFT = 6
_BLK_MASK = SEL_BLOCK - 1
VMEM_LIMIT = 48 * 1024 * 1024

_SLOPES = (2.0 ** (-8.0 * np.arange(1, 9) / 8)).reshape(N_GROUPS, N_REP)

_NT = (((1,), (1,)), ((), ()))


def _dot(a, b):
    return jnp.dot(a, b, preferred_element_type=F32)


def _dot_nt(a, b):
    return lax.dot_general(a, b, _NT, preferred_element_type=F32)


def _rmsnorm(x, g):
    ms = jnp.mean(x * x, axis=-1, keepdims=True)
    return x * lax.rsqrt(ms + NORM_EPS) * g


def _params(*sem):
    return pltpu.CompilerParams(dimension_semantics=sem, vmem_limit_bytes=VMEM_LIMIT)


FFN_CHUNK = 256
FFN_ROWS = 512


def _swiglu_half_step(x, g_ref, wi_ref, wo_ref):
    xn = _rmsnorm(x, g_ref[...]).astype(MXU_DTYPE)
    acc = None
    for f in range(D_FF // FFN_CHUNK):
        cols = slice(f * FFN_CHUNK, (f + 1) * FFN_CHUNK)
        gate = _dot(xn, wi_ref[:, cols])
        up = _dot(xn, wi_ref[:, D_FF + f * FFN_CHUNK:D_FF + (f + 1) * FFN_CHUNK])
        act = gate * jax.nn.sigmoid(gate) * up
        part = _dot(act.astype(MXU_DTYPE), wo_ref[cols, :])
        acc = part if acc is None else acc + part
    return x + 0.5 * acc


P16_W = 1024 + 6 * LANES
PROJ_W = P16_W + 3 * LANES


def _ffn_proj_kernel(x_ref, g_ref, wi_ref, wo_ref, pg_ref, pw_ref, h_ref, p16_ref, kc_ref, vc_ref, gt_ref):
    h = _swiglu_half_step(x_ref[...], g_ref, wi_ref, wo_ref)
    h_ref[...] = h
    y = _dot(_rmsnorm(h, pg_ref[...]).astype(MXU_DTYPE), pw_ref[...])
    p16_ref[:, 0:1024] = (y[:, 0:1024] * ATTN_SCALE).astype(MXU_DTYPE)
    p16_ref[:, 1024:P16_W] = y[:, 1024:P16_W].astype(MXU_DTYPE)
    kc_ref[...] = y[:, P16_W:P16_W + LANES]
    vc_ref[...] = y[:, P16_W + LANES:P16_W + 2 * LANES]
    gt_ref[...] = y[:, P16_W + 2 * LANES:PROJ_W]


def _resident(shape):
    return pl.BlockSpec(shape, lambda i: (0,) * len(shape), pipeline_mode=pl.Buffered(1))


def _ffn_proj(x2, norm_g, w_in, w_out, proj_g, proj_w):
    m = x2.shape[0]
    tm = min(FFN_ROWS, m)
    row = lambda i: (i, 0)
    return pl.pallas_call(
        _ffn_proj_kernel,
        grid=(m // tm,),
        in_specs=[
            pl.BlockSpec((tm, D_MODEL), row),
            _resident((1, D_MODEL)),
            _resident((D_MODEL, 2 * D_FF)),
            _resident((D_FF, D_MODEL)),
            _resident((1, D_MODEL)),
            _resident((D_MODEL, PROJ_W)),
        ],
        out_specs=[
            pl.BlockSpec((tm, D_MODEL), row),
            pl.BlockSpec((tm, P16_W), row),
            pl.BlockSpec((tm, LANES), row),
            pl.BlockSpec((tm, LANES), row),
            pl.BlockSpec((tm, LANES), row),
        ],
        out_shape=[
            jax.ShapeDtypeStruct((m, D_MODEL), F32),
            jax.ShapeDtypeStruct((m, P16_W), MXU_DTYPE),
            jax.ShapeDtypeStruct((m, LANES), F32),
            jax.ShapeDtypeStruct((m, LANES), F32),
            jax.ShapeDtypeStruct((m, LANES), F32),
        ],
        compiler_params=_params("arbitrary"),
        name="ffn_proj",
    )(x2, norm_g.reshape(1, D_MODEL), w_in, w_out, proj_g.reshape(1, D_MODEL), proj_w)


N_BRANCH = 3
GATE_EXP_W = N_BRANCH * N_REP * LANES


def _combine_ffn_kernel(h_ref, oc_ref, os_ref, ow_ref, ob_ref, gt_ref, e_ref, w_ref, g_ref, wi_ref, wo_ref, *rest,
                        final_norm):
    if final_norm:
        fg_ref, o_ref = rest
    else:
        (o_ref,) = rest
    sig = jax.nn.sigmoid(gt_ref[...])
    hi = sig.astype(MXU_DTYPE)
    lo = (sig - hi.astype(F32)).astype(MXU_DTYPE)
    gx = _dot(hi, e_ref[...]) + _dot(lo, e_ref[...])
    acc = h_ref[...]
    branches = (oc_ref, os_ref, ow_ref)
    for r in range(N_REP):
        sl = slice(r * LANES, (r + 1) * LANES)
        o_a = None
        for br in range(N_BRANCH):
            c = (br * N_REP + r) * LANES
            term = gx[:, c:c + LANES] * branches[br][:, sl]
            o_a = term if o_a is None else o_a + term
        acc = acc + _dot(o_a.astype(MXU_DTYPE), w_ref[sl, :])
    acc = acc + _dot(ob_ref[...].astype(MXU_DTYPE), w_ref[512:1024, :])
    h = _swiglu_half_step(acc, g_ref, wi_ref, wo_ref)
    if final_norm:
        h = _rmsnorm(h, fg_ref[...])
    o_ref[...] = h


def _gate_expand():
    e = np.zeros((LANES, GATE_EXP_W), np.float32)
    for br in range(N_BRANCH):
        for r in range(N_REP):
            for g in range(N_GROUPS):
                c0 = (br * N_REP + r) * LANES + g * HEAD_DIM
                e[br * 8 + r * 2 + g, c0:c0 + HEAD_DIM] = 1.0
    return e


def _combine_ffn(h2, o_cmp, o_slc, o_win, o_b, gates, w_out_p, norm_g, w_in, w_out, final_g=None):
    m = h2.shape[0]
    tm = min(FFN_ROWS, m)
    row = lambda i: (i, 0)
    final_norm = final_g is not None
    e = jnp.asarray(_gate_expand(), MXU_DTYPE)
    in_specs = [
        pl.BlockSpec((tm, D_MODEL), row),
        pl.BlockSpec((tm, 512), row),
        pl.BlockSpec((tm, 512), row),
        pl.BlockSpec((tm, 512), row),
        pl.BlockSpec((tm, 512), row),
        pl.BlockSpec((tm, LANES), row),
        _resident((LANES, GATE_EXP_W)),
        _resident((D_MODEL, D_MODEL)),
        _resident((1, D_MODEL)),
        _resident((D_MODEL, 2 * D_FF)),
        _resident((D_FF, D_MODEL)),
    ]
    args = [h2, o_cmp, o_slc, o_win, o_b, gates, e, w_out_p, norm_g.reshape(1, D_MODEL), w_in, w_out]
    if final_norm:
        in_specs.append(_resident((1, D_MODEL)))
        args.append(final_g.reshape(1, D_MODEL))
    return pl.pallas_call(
        functools.partial(_combine_ffn_kernel, final_norm=final_norm),
        grid=(m // tm,),
        in_specs=in_specs,
        out_specs=pl.BlockSpec((tm, D_MODEL), row),
        out_shape=jax.ShapeDtypeStruct((m, D_MODEL), F32),
        compiler_params=_params("arbitrary"),
        name="combine_ffn_final" if final_norm else "combine_ffn",
    )(*args)


def _gelu_tanh(x):
    return 0.5 * x * (1.0 + jnp.tanh(np.sqrt(2.0 / np.pi) * (x + 0.044715 * (x * x * x))))


def _compress_kernel(raw_ref, pa_ref, pb_ref, wa_ref, wb_ref, b1_ref, w2_ref, o_ref):
    x = raw_ref[0]
    n_rows = x.shape[0]
    first = _dot((x + pa_ref[...]).astype(MXU_DTYPE), wa_ref[...])
    second = _dot((x + pb_ref[...]).astype(MXU_DTYPE), wb_ref[...])
    hid = first + pltpu.roll(second, n_rows - 1, axis=0) + b1_ref[...]
    out = _dot(_gelu_tanh(hid).astype(MXU_DTYPE), w2_ref[...])
    row = lax.broadcasted_iota(jnp.int32, out.shape, 0)
    o_ref[0] = jnp.where(row == n_rows - 1, 0.0, out).astype(o_ref.dtype)


def _compress(raw, pos, w1, b1, w2):
    b, t, _ = raw.shape
    n_rows = t // CMP_STRIDE
    half = CMP_STRIDE
    kw = half * LANES
    eye = jnp.eye(N_GROUPS, dtype=F32)
    w1r = w1.reshape(CMP_BLOCK, HEAD_DIM, CMP_HIDDEN)

    def expand(wpart):
        return jnp.einsum("jdh,ab->jadbh", wpart, eye).reshape(kw, N_GROUPS * CMP_HIDDEN).astype(MXU_DTYPE)

    def expand_pos(ppart):
        return jnp.broadcast_to(ppart[:, None, :], (half, N_GROUPS, HEAD_DIM)).reshape(1, kw)

    wa, wb = expand(w1r[:half]), expand(w1r[half:])
    pa, pb = expand_pos(pos[:half]), expand_pos(pos[half:])
    b1e = jnp.tile(b1.reshape(1, CMP_HIDDEN), (1, N_GROUPS))
    w2e = jnp.einsum("hd,ab->ahbd", w2, eye).reshape(N_GROUPS * CMP_HIDDEN, LANES).astype(MXU_DTYPE)
    const = lambda i: (0, 0)
    return pl.pallas_call(
        _compress_kernel,
        grid=(b,),
        in_specs=[
            pl.BlockSpec((1, n_rows, kw), lambda i: (i, 0, 0)),
            pl.BlockSpec((1, kw), const),
            pl.BlockSpec((1, kw), const),
            pl.BlockSpec((kw, N_GROUPS * CMP_HIDDEN), const),
            pl.BlockSpec((kw, N_GROUPS * CMP_HIDDEN), const),
            pl.BlockSpec((1, N_GROUPS * CMP_HIDDEN), const),
            pl.BlockSpec((N_GROUPS * CMP_HIDDEN, LANES), const),
        ],
        out_specs=pl.BlockSpec((1, n_rows, LANES), lambda i: (i, 0, 0)),
        out_shape=jax.ShapeDtypeStruct((b, n_rows, LANES), MXU_DTYPE),
        compiler_params=_params("arbitrary"),
        name="compress",
    )(raw.reshape(b, n_rows, kw), pa, pb, wa, wb, b1e, w2e)


def _lane_half(shape):
    return lax.broadcasted_iota(jnp.int32, shape, len(shape) - 1) >> _BLK_SHIFT


def _alibi_cols(c0, c1, c2, c3, shape):
    lane = lax.broadcasted_iota(jnp.int32, shape, 1)
    z = jnp.zeros(shape, F32)
    return jnp.where(lane == 0, c0, jnp.where(lane == 1, c1, jnp.where(lane == 2, c2, jnp.where(lane == 3, c3, z))))


def _q_alibi(tq, t0, slope):
    t = t0 + lax.broadcasted_iota(jnp.int32, (tq, LANES), 0)
    hi = (t >> _BLK_SHIFT).astype(F32)
    lo = (t & _BLK_MASK).astype(F32)
    return _alibi_cols(slope * SEL_BLOCK * hi, slope * lo, slope * SEL_BLOCK, slope, (tq, LANES)).astype(MXU_DTYPE)


def _k_alibi(pos):
    hi = (pos >> _BLK_SHIFT).astype(F32)
    lo = (pos & _BLK_MASK).astype(F32)
    return _alibi_cols(-1.0, -1.0, hi, lo, pos.shape).astype(MXU_DTYPE)


def _build_qaug(qaug_scr, slab_of, t0, tq, other):
    half = _lane_half((tq, LANES))
    unit = _q_alibi(tq, t0, 1.0)
    for r in range(N_REP):
        slab = slab_of(r)
        for g in range(N_GROUPS):
            fill = jnp.zeros_like(slab) if other is None else other
            qaug_scr[g, r * tq:(r + 1) * tq, 0:LANES] = jnp.where(half == g, slab, fill)
            qaug_scr[g, r * tq:(r + 1) * tq, LANES:AUG_K] = unit * jnp.asarray(_SLOPES[g, r], MXU_DTYPE)


def _cmp_kernel(q_ref, kc_ref, vc_ref, ovl_ref, o_ref, ns_ref, need_ref, qaug_scr, score_scr, cnt_scr, *, tq, n_cmp, n_sel):
    i = pl.program_id(1)
    t0 = i * tq
    _build_qaug(qaug_scr, lambda r: q_ref[0, :, r * LANES:(r + 1) * LANES], t0, tq, None)

    kc = kc_ref[0]
    vc = vc_ref[0]
    half_k = _lane_half((n_cmp, LANES))
    c_pos = lax.broadcasted_iota(jnp.int32, (n_cmp, LANES), 0) * CMP_STRIDE + (CMP_BLOCK - 1)
    k_ali = _k_alibi(c_pos)
    rows = N_REP * tq
    c_end = lax.broadcasted_iota(jnp.int32, (n_cmp, rows), 0) * CMP_STRIDE + (CMP_BLOCK - 1)
    t_col = t0 + (lax.broadcasted_iota(jnp.int32, (n_cmp, rows), 1) & (tq - 1))
    vis = c_end <= t_col
    vc_t = jnp.transpose(vc.astype(F32)).astype(MXU_DTYPE)

    outs, psl_t = [], []
    for g in range(N_GROUPS):
        kaug = jnp.concatenate([jnp.where(half_k == g, kc, jnp.zeros_like(kc)), k_ali], axis=1)
        s = jnp.where(vis, _dot_nt(kaug, qaug_scr[g]), NEG_INF)
        m = jnp.max(s, axis=0, keepdims=True)
        e = jnp.where(vis, jnp.exp(s - m), 0.0)
        l = jnp.sum(e, axis=0, keepdims=True)
        pb = (e * jnp.where(l > 0.0, 1.0 / l, 0.0)).astype(MXU_DTYPE)
        outs.append(_dot(vc_t, pb))
        pt = _dot(ovl_ref[...], pb)
        psl_t.append(pt[:, 0:tq] + pt[:, tq:2 * tq] + pt[:, 2 * tq:3 * tq] + pt[:, 3 * tq:4 * tq])
    _emit_slabs(o_ref, 0, outs, tq)

    blk = lax.broadcasted_iota(jnp.int32, (SEL_BLOCK, tq), 0)
    cur = (t0 + lax.broadcasted_iota(jnp.int32, (SEL_BLOCK, tq), 1)) >> _BLK_SHIFT
    valid = blk <= cur
    forced = (blk == 0) | (blk == cur) | (blk == cur - 1)
    for n, g in enumerate((1, 0)):
        score_scr[n] = jnp.where(forced, FORCE_SCORE, jnp.where(valid, psl_t[g][0:SEL_BLOCK], NEG_INF))
    cnt_scr[...] = jnp.zeros_like(cnt_scr)

    n_valid = jnp.minimum(((t0 + tq - 1) >> _BLK_SHIFT) + 1, n_sel)
    chunk = 8
    for c0 in range(0, n_sel, chunk):
        @pl.when((n_valid > SEL_TOPK) & (c0 < n_valid))
        def _(c0=c0):
            for n in range(N_GROUPS):
                score = score_scr[n]
                cnt = cnt_scr[n]
                for j in range(c0, min(c0 + chunk, n_sel)):
                    xj = jnp.broadcast_to(score[j:j + 1, :], (SEL_BLOCK, tq))
                    beats = (xj > score) | ((xj == score) & (blk > j))
                    cnt = cnt + jnp.where(beats, 1, 0)
                cnt_scr[n] = cnt

    notsel = [jnp.where((cnt_scr[n] < SEL_TOPK) & valid, 0.0, 1.0) for n in range(N_GROUPS)]
    ns_t = jnp.transpose(jnp.concatenate(notsel, axis=0))
    ns_ref[0] = ns_t.astype(ns_ref.dtype)
    for u in range(tq // SEL_TILE):
        part = ns_t[u * SEL_TILE:(u + 1) * SEL_TILE]
        need_ref[0, u] = (1.0 - jnp.min(part, axis=0, keepdims=True)).astype(jnp.int32)


def _overlap_t(t):
    nc_pad = t // CMP_STRIDE
    ns = t // SEL_BLOCK
    c_start = np.arange(nc_pad) * CMP_STRIDE
    s_start = np.arange(ns) * SEL_BLOCK
    ov = np.clip(np.minimum(c_start[:, None] + CMP_BLOCK, s_start[None, :] + SEL_BLOCK)
                 - np.maximum(c_start[:, None], s_start[None, :]), 0, None) / CMP_BLOCK
    ov[nc_pad - 1] = 0.0
    out = np.zeros((LANES, nc_pad), np.float32)
    out[0:ns] = ov.T
    out[HEAD_DIM:HEAD_DIM + ns] = ov.T
    return out


def _cmp_attention(p16, kc, vc, *, tq=512):
    b, t, _ = p16.shape
    tq = min(tq, t)
    assert tq % SEL_TILE == 0
    n_cmp = t // CMP_STRIDE
    n_sel = t // SEL_BLOCK
    assert n_sel <= SEL_BLOCK and n_cmp % LANES == 0
    ovl = jnp.asarray(_overlap_t(t), MXU_DTYPE)
    return pl.pallas_call(
        functools.partial(_cmp_kernel, tq=tq, n_cmp=n_cmp, n_sel=n_sel),
        grid=(b, t // tq),
        in_specs=[
            pl.BlockSpec((1, tq, 512), lambda bi, i: (bi, i, 0)),
            pl.BlockSpec((1, n_cmp, LANES), lambda bi, i: (bi, 0, 0)),
            pl.BlockSpec((1, n_cmp, LANES), lambda bi, i: (bi, 0, 0)),
            pl.BlockSpec((LANES, n_cmp), lambda bi, i: (0, 0)),
        ],
        out_specs=[
            pl.BlockSpec((1, tq, 512), lambda bi, i: (bi, i, 0)),
            pl.BlockSpec((1, tq, LANES), lambda bi, i: (bi, i, 0)),
            pl.BlockSpec((1, tq // SEL_TILE, 1, LANES), lambda bi, i: (bi, i, 0, 0)),
        ],
        out_shape=[
            jax.ShapeDtypeStruct((b, t, 512), BRANCH_DTYPE),
            jax.ShapeDtypeStruct((b, t, LANES), MXU_DTYPE),
            jax.ShapeDtypeStruct((b, t // SEL_TILE, 1, LANES), jnp.int32),
        ],
        scratch_shapes=[
            pltpu.VMEM((N_GROUPS, N_REP * tq, AUG_K), MXU_DTYPE),
            pltpu.VMEM((N_GROUPS, SEL_BLOCK, tq), F32),
            pltpu.VMEM((N_GROUPS, SEL_BLOCK, tq), jnp.int32),
        ],
        compiler_params=_params("arbitrary", "arbitrary"),
        name="cmp_attn",
    )(p16, kc, vc, ovl)


def _build_kv(k_ref, v_ref, kaug_scr, vaug_scr, *, tile, seq, pad_tiles, select):
    half = _lane_half((tile, LANES))
    lane = lax.broadcasted_iota(jnp.int32, (tile, LANES), 1)
    for c in range(pad_tiles):
        for g in range(N_GROUPS):
            pad = jnp.where((half != g) & ((lane & _BLK_MASK) == 0), -MASK_BIG, 0.0).astype(MXU_DTYPE)
            kaug_scr[g, c * tile:(c + 1) * tile, 0:LANES] = pad
            kaug_scr[g, c * tile:(c + 1) * tile, LANES:AUG_K] = jnp.zeros((tile, LANES), MXU_DTYPE)
            vaug_scr[g, c] = jnp.zeros((LANES, tile), MXU_DTYPE)

    def body(c, carry):
        r0 = pl.multiple_of(c * tile, tile)
        k = k_ref[0, pl.ds(r0, tile), :]
        v = v_ref[0, pl.ds(r0, tile), :].astype(F32)
        pos = r0 + lax.broadcasted_iota(jnp.int32, (tile, LANES), 0)
        k_ali = _k_alibi(pos)
        if select:
            off = jnp.where((lane & _BLK_MASK) == (pos >> _BLK_SHIFT), -MASK_BIG, 0.0).astype(MXU_DTYPE)
        else:
            off = jnp.zeros((tile, LANES), MXU_DTYPE)
        dst = pl.multiple_of(r0 + pad_tiles * tile, tile)
        v_t = jnp.transpose(v)
        sub_half = lax.broadcasted_iota(jnp.int32, (LANES, tile), 0) >> _BLK_SHIFT
        for g in range(N_GROUPS):
            kaug_scr[g, pl.ds(dst, tile), 0:LANES] = jnp.where(half == g, k, off)
            kaug_scr[g, pl.ds(dst, tile), LANES:AUG_K] = k_ali
            vaug_scr[g, c + pad_tiles] = jnp.where(sub_half == g, v_t, 1.0).astype(MXU_DTYPE)
        return carry
    lax.fori_loop(0, seq // tile, body, 0)


def _mask_bias(tile, off, window):
    rows = N_REP * tile
    key = lax.broadcasted_iota(jnp.int32, (tile, rows), 0)
    qry = lax.broadcasted_iota(jnp.int32, (tile, rows), 1) & (tile - 1)
    dist = qry - key + off
    keep = dist >= 0
    if window is not None:
        keep = keep & (dist < window)
    return jnp.where(keep, 0.0, NEG_INF)


def _emit_slabs(o_ref, row0, per_group, tile):
    sub = lax.broadcasted_iota(jnp.int32, (LANES, N_REP * tile), 0)
    both = jnp.where(sub < HEAD_DIM, per_group[0], per_group[1])
    for r in range(N_REP):
        o_ref[0, row0:row0 + tile, r * LANES:(r + 1) * LANES] = jnp.transpose(
            both[:, r * tile:(r + 1) * tile]).astype(o_ref.dtype)


def _sel_kernel(need_ref, q_ref, ns_ref, k_ref, v_ref, o_ref, kaug_scr, vaug_scr, qaug_scr, s_a, s_b, s_c, bias_scr,
                acc_scr, m_scr, todo_scr, *, tile, seq):
    i = pl.program_id(1)
    t0 = i * tile
    n_q = seq // tile

    base = (pl.program_id(0) * n_q + i) * n_q

    def scan(j, n):
        wanted = need_ref[base + j] > 0

        @pl.when(wanted)
        def _():
            todo_scr[n] = j
        return n + wanted.astype(jnp.int32)
    n_todo = lax.fori_loop(0, i, scan, 0)
    todo_scr[n_todo] = i

    @pl.when(i == 0)
    def _():
        _build_kv(k_ref, v_ref, kaug_scr, vaug_scr, tile=tile, seq=seq, pad_tiles=0, select=True)
        bias_scr[...] = _mask_bias(tile, 0, None)

    _build_qaug(qaug_scr, lambda r: q_ref[0, :, r * LANES:(r + 1) * LANES], t0, tile, ns_ref[0])
    acc_scr[...] = jnp.zeros_like(acc_scr)
    m_scr[...] = jnp.full_like(m_scr, -3e38)

    def scores_into(s_buf, j):
        k0 = pl.multiple_of(j * tile, tile)
        for g in range(N_GROUPS):
            s_buf[g] = _dot_nt(kaug_scr[g, pl.ds(k0, tile), :], qaug_scr[g])

    def consume(s_buf, j, diagonal):
        for g in range(N_GROUPS):
            s = s_buf[g]
            if diagonal:
                s = s + bias_scr[...]
            m_old = m_scr[g]
            m_new = jnp.maximum(m_old, jnp.max(s, axis=0, keepdims=True))
            p = jnp.exp(s - m_new).astype(MXU_DTYPE)
            acc_scr[g] = acc_scr[g] * jnp.exp(m_old - m_new) + _dot(vaug_scr[g, j], p)
            m_scr[g] = m_new

    scores_into(s_a, todo_scr[0])
    n_trip = n_todo // 3

    def triple(tt, carry):
        e = 3 * tt
        first, second, third = todo_scr[e], todo_scr[e + 1], todo_scr[e + 2]
        scores_into(s_b, second)
        consume(s_a, first, False)
        scores_into(s_c, third)
        consume(s_b, second, False)
        scores_into(s_a, todo_scr[e + 3])
        consume(s_c, third, False)
        return carry
    lax.fori_loop(0, n_trip, triple, 0)
    e0 = 3 * n_trip
    left = n_todo - e0

    @pl.when(left == 0)
    def _():
        consume(s_a, i, True)

    @pl.when(left == 1)
    def _():
        scores_into(s_b, i)
        consume(s_a, todo_scr[e0], False)
        consume(s_b, i, True)

    @pl.when(left == 2)
    def _():
        second = todo_scr[e0 + 1]
        scores_into(s_b, second)
        consume(s_a, todo_scr[e0], False)
        scores_into(s_c, i)
        consume(s_b, second, False)
        consume(s_c, i, True)

    outs = []
    for g in range(N_GROUPS):
        acc = acc_scr[g]
        den_row = (1 - g) * HEAD_DIM
        outs.append(acc / acc[den_row:den_row + 1, :])
    _emit_slabs(o_ref, 0, outs, tile)


def _sel_attention(p16, ns, block_need, q_blk, k_blk, v_blk, *, tile):
    b, t, _ = p16.shape
    tile = min(tile, t)
    n_q = t // tile
    rows = N_REP * tile
    assert block_need.shape[1] == n_q
    per_block = jnp.maximum(block_need[:, :, 0, 0:SEL_BLOCK], block_need[:, :, 0, SEL_BLOCK:])[:, :, 0:t // SEL_BLOCK]
    tile_need = per_block.reshape(b, n_q, n_q, tile // SEL_BLOCK).max(axis=-1).reshape(-1)
    grid_spec = pltpu.PrefetchScalarGridSpec(
        num_scalar_prefetch=1,
        grid=(b, n_q),
        in_specs=[
            pl.BlockSpec((1, tile, 512), lambda bi, i, need: (bi, i, q_blk)),
            pl.BlockSpec((1, tile, LANES), lambda bi, i, need: (bi, i, 0)),
            pl.BlockSpec((1, t, LANES), lambda bi, i, need: (bi, 0, k_blk)),
            pl.BlockSpec((1, t, LANES), lambda bi, i, need: (bi, 0, v_blk)),
        ],
        out_specs=pl.BlockSpec((1, tile, 512), lambda bi, i, need: (bi, i, 0)),
        scratch_shapes=[
            pltpu.VMEM((N_GROUPS, t, AUG_K), MXU_DTYPE),
            pltpu.VMEM((N_GROUPS, t // tile, LANES, tile), MXU_DTYPE),
            pltpu.VMEM((N_GROUPS, rows, AUG_K), MXU_DTYPE),
            pltpu.VMEM((N_GROUPS, tile, rows), F32),
            pltpu.VMEM((N_GROUPS, tile, rows), F32),
            pltpu.VMEM((N_GROUPS, tile, rows), F32),
            pltpu.VMEM((tile, rows), F32),
            pltpu.VMEM((N_GROUPS, LANES, rows), F32),
            pltpu.VMEM((N_GROUPS, 1, rows), F32),
            pltpu.SMEM((n_q + 1,), jnp.int32),
        ],
    )
    return pl.pallas_call(
        functools.partial(_sel_kernel, tile=tile, seq=t),
        grid_spec=grid_spec,
        out_shape=jax.ShapeDtypeStruct((b, t, 512), BRANCH_DTYPE),
        compiler_params=_params("arbitrary", "arbitrary"),
        name="attn_sel",
    )(tile_need, p16, ns, p16, p16)


def _band_kernel(*refs, tile, seq, n_prev, window, has_sink):
    refs = list(refs)
    sink_ref = refs.pop(0) if has_sink else None
    q_ref, qn_ref, k_ref, v_ref, o_ref, kaug_scr, vaug_scr, qaug_a, qaug_b, s_a, s_b, bias_scr = refs
    i2 = pl.program_id(1)
    n_q = seq // tile
    span = n_prev + 1
    rows = N_REP * tile
    lane = lax.broadcasted_iota(jnp.int32, (tile, LANES), 1)
    pad_flag = jnp.where((lane & _BLK_MASK) == 0, 1.0, 0.0).astype(MXU_DTYPE)
    offs = [(n_prev - n) * tile for n in range(span)]
    masked = [n for n in range(span) if offs[n] - (tile - 1) < 0 or offs[n] + tile - 1 >= window]

    def scores_into(s_buf, qaug_buf, qt):
        k0 = pl.multiple_of(qt * tile, tile)
        for g in range(N_GROUPS):
            s = _dot_nt(kaug_scr[g, pl.ds(k0, span * tile), :], qaug_buf[g])
            for n in range(span):
                part = s[n * tile:(n + 1) * tile]
                s_buf[g, n * tile:(n + 1) * tile, :] = part + bias_scr[masked.index(n)] if n in masked else part

    def finish(s_buf, qt, row0):
        outs = []
        for g in range(N_GROUPS):
            def piece(n):
                return s_buf[g, n * tile:(n + 1) * tile, :]
            m = jnp.max(piece(0), axis=0, keepdims=True)
            for n in range(1, span):
                m = jnp.maximum(m, jnp.max(piece(n), axis=0, keepdims=True))
            pv = None
            for n in range(span):
                p = jnp.exp(piece(n) - m).astype(MXU_DTYPE)
                d = _dot(vaug_scr[g, qt + n], p)
                pv = d if pv is None else pv + d
            den_row = (1 - g) * HEAD_DIM
            den = pv[den_row:den_row + 1, :]
            if has_sink:
                col = lax.broadcasted_iota(jnp.int32, (1, rows), 1)
                sink = jnp.zeros((1, rows), F32)
                for r in range(N_REP):
                    sink = jnp.where(col >= r * tile, sink_ref[g * N_REP + r], sink)
                m2 = jnp.maximum(m, sink)
                scale = jnp.exp(m - m2)
                pv = pv * scale
                den = den * scale + jnp.exp(sink - m2)
            outs.append(pv / den)
        _emit_slabs(o_ref, row0, outs, tile)

    @pl.when(i2 == 0)
    def _():
        _build_kv(k_ref, v_ref, kaug_scr, vaug_scr, tile=tile, seq=seq, pad_tiles=n_prev, select=False)
        for idx, n in enumerate(masked):
            bias_scr[idx] = _mask_bias(tile, offs[n], window)
        _build_qaug(qaug_a, lambda r: q_ref[0, 0:tile, r * LANES:(r + 1) * LANES], 0, tile, pad_flag)
        scores_into(s_a, qaug_a, 0)

    qt = 2 * i2
    _build_qaug(qaug_b, lambda r: q_ref[0, tile:2 * tile, r * LANES:(r + 1) * LANES], (qt + 1) * tile, tile, pad_flag)
    finish(s_a, qt, 0)
    scores_into(s_b, qaug_b, qt + 1)
    qn = jnp.minimum(qt + 2, n_q - 1)
    _build_qaug(qaug_a, lambda r: qn_ref[0, :, r * LANES:(r + 1) * LANES], qn * tile, tile, pad_flag)
    scores_into(s_a, qaug_a, qn)
    finish(s_b, qt + 1, tile)


def _band_attention(p16, q_blk, k_blk, v_blk, *, tile, window, sinks=None):
    b, t, _ = p16.shape
    tile = min(tile, t // 2)
    n_q = t // tile
    n_prev = -(-(window - 1) // tile)
    span = n_prev + 1
    assert n_q % 2 == 0
    has_sink = sinks is not None
    rows = N_REP * tile
    offs = [(n_prev - n) * tile for n in range(span)]
    n_masked = sum(1 for o in offs if o - (tile - 1) < 0 or o + tile - 1 >= window)
    in_specs, args = [], []
    if has_sink:
        in_specs.append(pl.BlockSpec(memory_space=pltpu.SMEM))
        args.append(sinks.astype(F32))
    in_specs += [
        pl.BlockSpec((1, 2 * tile, 512), lambda bi, i: (bi, i, q_blk)),
        pl.BlockSpec((1, tile, 512), lambda bi, i: (bi, jnp.minimum(2 * i + 2, n_q - 1), q_blk)),
        pl.BlockSpec((1, t, LANES), lambda bi, i: (bi, 0, k_blk)),
        pl.BlockSpec((1, t, LANES), lambda bi, i: (bi, 0, v_blk)),
    ]
    args += [p16, p16, p16, p16]
    return pl.pallas_call(
        functools.partial(_band_kernel, tile=tile, seq=t, n_prev=n_prev, window=window, has_sink=has_sink),
        grid=(b, n_q // 2),
        in_specs=in_specs,
        out_specs=pl.BlockSpec((1, 2 * tile, 512), lambda bi, i: (bi, i, 0)),
        out_shape=jax.ShapeDtypeStruct((b, t, 512), BRANCH_DTYPE),
        scratch_shapes=[
            pltpu.VMEM((N_GROUPS, t + n_prev * tile, AUG_K), MXU_DTYPE),
            pltpu.VMEM((N_GROUPS, n_q + n_prev, LANES, tile), MXU_DTYPE),
            pltpu.VMEM((N_GROUPS, rows, AUG_K), MXU_DTYPE),
            pltpu.VMEM((N_GROUPS, rows, AUG_K), MXU_DTYPE),
            pltpu.VMEM((N_GROUPS, span * tile, rows), F32),
            pltpu.VMEM((N_GROUPS, span * tile, rows), F32),
            pltpu.VMEM((n_masked, tile, rows), F32),
        ],
        compiler_params=_params("arbitrary", "arbitrary"),
        name="attn_band" + ("_sink" if has_sink else ""),
    )(*args)


def _slab_cols(w):
    lead = w.shape[:-1]
    return w.reshape(*lead, N_GROUPS, N_REP, HEAD_DIM).swapaxes(-3, -2).reshape(*lead, N_GROUPS * N_REP * HEAD_DIM)


def _arrange_proj_weight(w):
    q_a = _slab_cols(w[:, 0:512])
    kc, vc, ks, vs, kw, vw = [w[:, 512 + j * LANES:512 + (j + 1) * LANES] for j in range(6)]
    gates = w[:, 1280:1304].reshape(-1, N_GROUPS, N_REP, N_BRANCH).transpose(0, 3, 2, 1).reshape(-1, 24)
    gates = jnp.pad(gates, ((0, 0), (0, LANES - 24)))
    q_b = _slab_cols(w[:, 1304:1816])
    k_b, v_b = w[:, 1816:1944], w[:, 1944:2072]
    return jnp.concatenate([q_a, q_b, ks, vs, kw, vw, k_b, v_b, kc, vc, gates], axis=1).astype(MXU_DTYPE)


def _arrange_out_weight(w):
    w_a = _slab_cols(w[0:512].T).T
    w_b = _slab_cols(w[512:1024].T).T
    return jnp.concatenate([w_a, w_b], axis=0).astype(MXU_DTYPE)


_QA_BLK, _QB_BLK = 0, 1
_KS_BLK, _VS_BLK, _KW_BLK, _VW_BLK, _KB_BLK, _VB_BLK = 8, 9, 10, 11, 12, 13


def kernel(x, ffn1_norm, ffn1_w_in, ffn1_w_out, mix_norm, w_mix_in, cmp_k_pos, cmp_k_w1, cmp_k_b1, cmp_k_w2, cmp_v_pos, cmp_v_w1, cmp_v_b1, cmp_v_w2, swa_sinks, w_mix_out, ffn2_norm, ffn2_w_in, ffn2_w_out, final_norm):
    b, t, d = x.shape
    m = b * t
    depth = ffn1_norm.shape[0]
    h = x.reshape(m, d)
    for l in range(depth):
        last = l == depth - 1
        h, p16, kc_raw, vc_raw, gates = _ffn_proj(
            h, ffn1_norm[l], ffn1_w_in[l].astype(MXU_DTYPE), ffn1_w_out[l].astype(MXU_DTYPE),
            mix_norm[l], _arrange_proj_weight(w_mix_in[l]))
        p16 = p16.reshape(b, t, P16_W)
        kc = _compress(kc_raw.reshape(b, t, LANES), cmp_k_pos[l], cmp_k_w1[l], cmp_k_b1[l], cmp_k_w2[l])
        vc = _compress(vc_raw.reshape(b, t, LANES), cmp_v_pos[l], cmp_v_w1[l], cmp_v_b1[l], cmp_v_w2[l])
        o_cmp, notsel, block_need = _cmp_attention(p16, kc, vc)
        o_slc = _sel_attention(p16, notsel, block_need, _QA_BLK, _KS_BLK, _VS_BLK, tile=SEL_TILE)
        o_win = _band_attention(p16, _QA_BLK, _KW_BLK, _VW_BLK, tile=256, window=NSA_WINDOW)
        o_b = _band_attention(p16, _QB_BLK, _KB_BLK, _VB_BLK, tile=128, window=SWA_WINDOW, sinks=swa_sinks[l])
        h = _combine_ffn(
            h, o_cmp.reshape(m, 512), o_slc.reshape(m, 512), o_win.reshape(m, 512), o_b.reshape(m, 512), gates,
            _arrange_out_weight(w_mix_out[l]), ffn2_norm[l], ffn2_w_in[l].astype(MXU_DTYPE),
            ffn2_w_out[l].astype(MXU_DTYPE), final_norm if last else None)
    if depth == 0:
        raise ValueError("depth must be positive")
    return h.reshape(b, t, d)
```

```python
import functools

import numpy as np
import jax
import jax.numpy as jnp
from jax import lax
from jax.experimental import pallas as pl
from jax.experimental.pallas import tpu as pltpu

F32 = jnp.float32
MXU_DTYPE = jnp.bfloat16
BRANCH_DTYPE = jnp.bfloat16

D_MODEL = 1024
HEAD_DIM = 64
N_GROUPS = 2
N_REP = 4
CMP_BLOCK = 32
CMP_STRIDE = 16
CMP_HIDDEN = 256
SEL_BLOCK = 64
SEL_TOPK = 16
NSA_WINDOW = 512
SWA_WINDOW = 128
SEL_TILE = 256
D_FF = 2816
NORM_EPS = 1e-6
NEG_INF = -1e30
FORCE_SCORE = 1e9
ATTN_SCALE = HEAD_DIM ** -0.5
MASK_BIG = 2.0 ** 100
LANES = 128
AUG_K = 2 * LANES
_BLK_SHIFT = 6
_BLK_MASK = SEL_BLOCK - 1
VMEM_LIMIT = 48 * 1024 * 1024

_SLOPES = (2.0 ** (-8.0 * np.arange(1, 9) / 8)).reshape(N_GROUPS, N_REP)

_NT = (((1,), (1,)), ((), ()))


def _dot(a, b):
    return jnp.dot(a, b, preferred_element_type=F32)


def _dot_nt(a, b):
    return lax.dot_general(a, b, _NT, preferred_element_type=F32)


def _rmsnorm(x, g):
    ms = jnp.mean(x * x, axis=-1, keepdims=True)
    return x * lax.rsqrt(ms + NORM_EPS) * g


def _params(*sem):
    return pltpu.CompilerParams(dimension_semantics=sem, vmem_limit_bytes=VMEM_LIMIT)


FFN_CHUNK = 256
FFN_ROWS = 512


def _swiglu_half_step(x, g_ref, wi_ref, wo_ref):
    xn = _rmsnorm(x, g_ref[...]).astype(MXU_DTYPE)
    acc = None
    for f in range(D_FF // FFN_CHUNK):
        cols = slice(f * FFN_CHUNK, (f + 1) * FFN_CHUNK)
        gate = _dot(xn, wi_ref[:, cols])
        up = _dot(xn, wi_ref[:, D_FF + f * FFN_CHUNK:D_FF + (f + 1) * FFN_CHUNK])
        act = gate * jax.nn.sigmoid(gate) * up
        part = _dot(act.astype(MXU_DTYPE), wo_ref[cols, :])
        acc = part if acc is None else acc + part
    return x + 0.5 * acc


P16_W = 1024 + 6 * LANES
PROJ_W = P16_W + 3 * LANES


def _ffn_proj_kernel(x_ref, g_ref, wi_ref, wo_ref, pg_ref, pw_ref, h_ref, p16_ref, kc_ref, vc_ref, gt_ref):
    h = _swiglu_half_step(x_ref[...], g_ref, wi_ref, wo_ref)
    h_ref[...] = h
    y = _dot(_rmsnorm(h, pg_ref[...]).astype(MXU_DTYPE), pw_ref[...])
    p16_ref[:, 0:1024] = (y[:, 0:1024] * ATTN_SCALE).astype(MXU_DTYPE)
    p16_ref[:, 1024:P16_W] = y[:, 1024:P16_W].astype(MXU_DTYPE)
    kc_ref[...] = y[:, P16_W:P16_W + LANES]
    vc_ref[...] = y[:, P16_W + LANES:P16_W + 2 * LANES]
    gt_ref[...] = y[:, P16_W + 2 * LANES:PROJ_W]


def _resident(shape):
    return pl.BlockSpec(shape, lambda i: (0,) * len(shape), pipeline_mode=pl.Buffered(1))


def _ffn_proj(x2, norm_g, w_in, w_out, proj_g, proj_w):
    m = x2.shape[0]
    tm = min(FFN_ROWS, m)
    row = lambda i: (i, 0)
    return pl.pallas_call(
        _ffn_proj_kernel,
        grid=(m // tm,),
        in_specs=[
            pl.BlockSpec((tm, D_MODEL), row),
            _resident((1, D_MODEL)),
            _resident((D_MODEL, 2 * D_FF)),
            _resident((D_FF, D_MODEL)),
            _resident((1, D_MODEL)),
            _resident((D_MODEL, PROJ_W)),
        ],
        out_specs=[
            pl.BlockSpec((tm, D_MODEL), row),
            pl.BlockSpec((tm, P16_W), row),
            pl.BlockSpec((tm, LANES), row),
            pl.BlockSpec((tm, LANES), row),
            pl.BlockSpec((tm, LANES), row),
        ],
        out_shape=[
            jax.ShapeDtypeStruct((m, D_MODEL), F32),
            jax.ShapeDtypeStruct((m, P16_W), MXU_DTYPE),
            jax.ShapeDtypeStruct((m, LANES), F32),
            jax.ShapeDtypeStruct((m, LANES), F32),
            jax.ShapeDtypeStruct((m, LANES), F32),
        ],
        compiler_params=_params("arbitrary"),
        name="ffn_proj",
    )(x2, norm_g.reshape(1, D_MODEL), w_in, w_out, proj_g.reshape(1, D_MODEL), proj_w)


N_BRANCH = 3
GATE_EXP_W = N_BRANCH * N_REP * LANES


def _combine_ffn_kernel(h_ref, oc_ref, os_ref, ow_ref, ob_ref, gt_ref, e_ref, w_ref, g_ref, wi_ref, wo_ref, *rest,
                        final_norm):
    if final_norm:
        fg_ref, o_ref = rest
    else:
        (o_ref,) = rest
    sig = jax.nn.sigmoid(gt_ref[...])
    hi = sig.astype(MXU_DTYPE)
    lo = (sig - hi.astype(F32)).astype(MXU_DTYPE)
    gx = _dot(hi, e_ref[...]) + _dot(lo, e_ref[...])
    acc = h_ref[...]
    branches = (oc_ref, os_ref, ow_ref)
    for r in range(N_REP):
        sl = slice(r * LANES, (r + 1) * LANES)
        o_a = None
        for br in range(N_BRANCH):
            c = (br * N_REP + r) * LANES
            term = gx[:, c:c + LANES] * branches[br][:, sl]
            o_a = term if o_a is None else o_a + term
        acc = acc + _dot(o_a.astype(MXU_DTYPE), w_ref[sl, :])
    acc = acc + _dot(ob_ref[...].astype(MXU_DTYPE), w_ref[512:1024, :])
    h = _swiglu_half_step(acc, g_ref, wi_ref, wo_ref)
    if final_norm:
        h = _rmsnorm(h, fg_ref[...])
    o_ref[...] = h


def _gate_expand():
    e = np.zeros((LANES, GATE_EXP_W), np.float32)
    for br in range(N_BRANCH):
        for r in range(N_REP):
            for g in range(N_GROUPS):
                c0 = (br * N_REP + r) * LANES + g * HEAD_DIM
                e[br * 8 + r * 2 + g, c0:c0 + HEAD_DIM] = 1.0
    return e


def _combine_ffn(h2, o_cmp, o_slc, o_win, o_b, gates, w_out_p, norm_g, w_in, w_out, final_g=None):
    m = h2.shape[0]
    tm = min(FFN_ROWS, m)
    row = lambda i: (i, 0)
    final_norm = final_g is not None
    e = jnp.asarray(_gate_expand(), MXU_DTYPE)
    in_specs = [
        pl.BlockSpec((tm, D_MODEL), row),
        pl.BlockSpec((tm, 512), row),
        pl.BlockSpec((tm, 512), row),
        pl.BlockSpec((tm, 512), row),
        pl.BlockSpec((tm, 512), row),
        pl.BlockSpec((tm, LANES), row),
        _resident((LANES, GATE_EXP_W)),
        _resident((D_MODEL, D_MODEL)),
        _resident((1, D_MODEL)),
        _resident((D_MODEL, 2 * D_FF)),
        _resident((D_FF, D_MODEL)),
    ]
    args = [h2, o_cmp, o_slc, o_win, o_b, gates, e, w_out_p, norm_g.reshape(1, D_MODEL), w_in, w_out]
    if final_norm:
        in_specs.append(_resident((1, D_MODEL)))
        args.append(final_g.reshape(1, D_MODEL))
    return pl.pallas_call(
        functools.partial(_combine_ffn_kernel, final_norm=final_norm),
        grid=(m // tm,),
        in_specs=in_specs,
        out_specs=pl.BlockSpec((tm, D_MODEL), row),
        out_shape=jax.ShapeDtypeStruct((m, D_MODEL), F32),
        compiler_params=_params("arbitrary"),
        name="combine_ffn_final" if final_norm else "combine_ffn",
    )(*args)


def _gelu_tanh(x):
    return 0.5 * x * (1.0 + jnp.tanh(np.sqrt(2.0 / np.pi) * (x + 0.044715 * (x * x * x))))


def _compress_kernel(raw_ref, pa_ref, pb_ref, wa_ref, wb_ref, b1_ref, w2_ref, o_ref):
    x = raw_ref[0]
    n_rows = x.shape[0]
    first = _dot((x + pa_ref[...]).astype(MXU_DTYPE), wa_ref[...])
    second = _dot((x + pb_ref[...]).astype(MXU_DTYPE), wb_ref[...])
    hid = first + pltpu.roll(second, n_rows - 1, axis=0) + b1_ref[...]
    out = _dot(_gelu_tanh(hid).astype(MXU_DTYPE), w2_ref[...])
    row = lax.broadcasted_iota(jnp.int32, out.shape, 0)
    o_ref[0] = jnp.where(row == n_rows - 1, 0.0, out).astype(o_ref.dtype)


def _compress(raw, pos, w1, b1, w2):
    b, t, _ = raw.shape
    n_rows = t // CMP_STRIDE
    half = CMP_STRIDE
    kw = half * LANES
    eye = jnp.eye(N_GROUPS, dtype=F32)
    w1r = w1.reshape(CMP_BLOCK, HEAD_DIM, CMP_HIDDEN)

    def expand(wpart):
        return jnp.einsum("jdh,ab->jadbh", wpart, eye).reshape(kw, N_GROUPS * CMP_HIDDEN).astype(MXU_DTYPE)

    def expand_pos(ppart):
        return jnp.broadcast_to(ppart[:, None, :], (half, N_GROUPS, HEAD_DIM)).reshape(1, kw)

    wa, wb = expand(w1r[:half]), expand(w1r[half:])
    pa, pb = expand_pos(pos[:half]), expand_pos(pos[half:])
    b1e = jnp.tile(b1.reshape(1, CMP_HIDDEN), (1, N_GROUPS))
    w2e = jnp.einsum("hd,ab->ahbd", w2, eye).reshape(N_GROUPS * CMP_HIDDEN, LANES).astype(MXU_DTYPE)
    const = lambda i: (0, 0)
    return pl.pallas_call(
        _compress_kernel,
        grid=(b,),
        in_specs=[
            pl.BlockSpec((1, n_rows, kw), lambda i: (i, 0, 0)),
            pl.BlockSpec((1, kw), const),
            pl.BlockSpec((1, kw), const),
            pl.BlockSpec((kw, N_GROUPS * CMP_HIDDEN), const),
            pl.BlockSpec((kw, N_GROUPS * CMP_HIDDEN), const),
            pl.BlockSpec((1, N_GROUPS * CMP_HIDDEN), const),
            pl.BlockSpec((N_GROUPS * CMP_HIDDEN, LANES), const),
        ],
        out_specs=pl.BlockSpec((1, n_rows, LANES), lambda i: (i, 0, 0)),
        out_shape=jax.ShapeDtypeStruct((b, n_rows, LANES), MXU_DTYPE),
        compiler_params=_params("arbitrary"),
        name="compress",
    )(raw.reshape(b, n_rows, kw), pa, pb, wa, wb, b1e, w2e)


def _lane_half(shape):
    return lax.broadcasted_iota(jnp.int32, shape, len(shape) - 1) >> _BLK_SHIFT


def _alibi_cols(c0, c1, c2, c3, shape):
    lane = lax.broadcasted_iota(jnp.int32, shape, 1)
    z = jnp.zeros(shape, F32)
    return jnp.where(lane == 0, c0, jnp.where(lane == 1, c1, jnp.where(lane == 2, c2, jnp.where(lane == 3, c3, z))))


def _q_alibi(tq, t0, slope):
    t = t0 + lax.broadcasted_iota(jnp.int32, (tq, LANES), 0)
    hi = (t >> _BLK_SHIFT).astype(F32)
    lo = (t & _BLK_MASK).astype(F32)
    return _alibi_cols(slope * SEL_BLOCK * hi, slope * lo, slope * SEL_BLOCK, slope, (tq, LANES)).astype(MXU_DTYPE)


def _k_alibi(pos):
    hi = (pos >> _BLK_SHIFT).astype(F32)
    lo = (pos & _BLK_MASK).astype(F32)
    return _alibi_cols(-1.0, -1.0, hi, lo, pos.shape).astype(MXU_DTYPE)


def _build_qaug(qaug_scr, slab_of, t0, tq, other):
    half = _lane_half((tq, LANES))
    unit = _q_alibi(tq, t0, 1.0)
    for r in range(N_REP):
        slab = slab_of(r)
        for g in range(N_GROUPS):
            fill = jnp.zeros_like(slab) if other is None else other
            qaug_scr[g, r * tq:(r + 1) * tq, 0:LANES] = jnp.where(half == g, slab, fill)
            qaug_scr[g, r * tq:(r + 1) * tq, LANES:AUG_K] = unit * jnp.asarray(_SLOPES[g, r], MXU_DTYPE)


def _cmp_kernel(q_ref, kc_ref, vc_ref, ovl_ref, o_ref, ns_ref, need_ref, qaug_scr, score_scr, cnt_scr, *, tq, n_cmp, n_sel):
    i = pl.program_id(1)
    t0 = i * tq
    _build_qaug(qaug_scr, lambda r: q_ref[0, :, r * LANES:(r + 1) * LANES], t0, tq, None)

    kc = kc_ref[0]
    vc = vc_ref[0]
    half_k = _lane_half((n_cmp, LANES))
    c_pos = lax.broadcasted_iota(jnp.int32, (n_cmp, LANES), 0) * CMP_STRIDE + (CMP_BLOCK - 1)
    k_ali = _k_alibi(c_pos)
    rows = N_REP * tq
    c_end = lax.broadcasted_iota(jnp.int32, (n_cmp, rows), 0) * CMP_STRIDE + (CMP_BLOCK - 1)
    t_col = t0 + (lax.broadcasted_iota(jnp.int32, (n_cmp, rows), 1) & (tq - 1))
    vis = c_end <= t_col
    vc_t = jnp.transpose(vc.astype(F32)).astype(MXU_DTYPE)

    outs, psl_t = [], []
    for g in range(N_GROUPS):
        kaug = jnp.concatenate([jnp.where(half_k == g, kc, jnp.zeros_like(kc)), k_ali], axis=1)
        s = jnp.where(vis, _dot_nt(kaug, qaug_scr[g]), NEG_INF)
        m = jnp.max(s, axis=0, keepdims=True)
        e = jnp.where(vis, jnp.exp(s - m), 0.0)
        l = jnp.sum(e, axis=0, keepdims=True)
        pb = (e * jnp.where(l > 0.0, 1.0 / l, 0.0)).astype(MXU_DTYPE)
        outs.append(_dot(vc_t, pb))
        pt = _dot(ovl_ref[...], pb)
        psl_t.append(pt[:, 0:tq] + pt[:, tq:2 * tq] + pt[:, 2 * tq:3 * tq] + pt[:, 3 * tq:4 * tq])
    _emit_slabs(o_ref, 0, outs, tq)

    blk = lax.broadcasted_iota(jnp.int32, (SEL_BLOCK, tq), 0)
    cur = (t0 + lax.broadcasted_iota(jnp.int32, (SEL_BLOCK, tq), 1)) >> _BLK_SHIFT
    valid = blk <= cur
    forced = (blk == 0) | (blk == cur) | (blk == cur - 1)
    for n, g in enumerate((1, 0)):
        score_scr[n] = jnp.where(forced, FORCE_SCORE, jnp.where(valid, psl_t[g][0:SEL_BLOCK], NEG_INF))
    cnt_scr[...] = jnp.zeros_like(cnt_scr)

    n_valid = jnp.minimum(((t0 + tq - 1) >> _BLK_SHIFT) + 1, n_sel)
    chunk = 8
    for c0 in range(0, n_sel, chunk):
        @pl.when((n_valid > SEL_TOPK) & (c0 < n_valid))
        def _(c0=c0):
            for n in range(N_GROUPS):
                score = score_scr[n]
                cnt = cnt_scr[n]
                for j in range(c0, min(c0 + chunk, n_sel)):
                    xj = jnp.broadcast_to(score[j:j + 1, :], (SEL_BLOCK, tq))
                    beats = (xj > score) | ((xj == score) & (blk > j))
                    cnt = cnt + jnp.where(beats, 1, 0)
                cnt_scr[n] = cnt

    notsel = [jnp.where((cnt_scr[n] < SEL_TOPK) & valid, 0.0, 1.0) for n in range(N_GROUPS)]
    ns_t = jnp.transpose(jnp.concatenate(notsel, axis=0))
    ns_ref[0] = ns_t.astype(ns_ref.dtype)
    for u in range(tq // SEL_TILE):
        part = ns_t[u * SEL_TILE:(u + 1) * SEL_TILE]
        need_ref[0, u] = (1.0 - jnp.min(part, axis=0, keepdims=True)).astype(jnp.int32)


def _overlap_t(t):
    nc_pad = t // CMP_STRIDE
    ns = t // SEL_BLOCK
    c_start = np.arange(nc_pad) * CMP_STRIDE
    s_start = np.arange(ns) * SEL_BLOCK
    ov = np.clip(np.minimum(c_start[:, None] + CMP_BLOCK, s_start[None, :] + SEL_BLOCK)
                 - np.maximum(c_start[:, None], s_start[None, :]), 0, None) / CMP_BLOCK
    ov[nc_pad - 1] = 0.0
    out = np.zeros((LANES, nc_pad), np.float32)
    out[0:ns] = ov.T
    out[HEAD_DIM:HEAD_DIM + ns] = ov.T
    return out


def _cmp_attention(p16, kc, vc, *, tq=512):
    b, t, _ = p16.shape
    tq = min(tq, t)
    assert tq % SEL_TILE == 0
    n_cmp = t // CMP_STRIDE
    n_sel = t // SEL_BLOCK
    assert n_sel <= SEL_BLOCK and n_cmp % LANES == 0
    ovl = jnp.asarray(_overlap_t(t), MXU_DTYPE)
    return pl.pallas_call(
        functools.partial(_cmp_kernel, tq=tq, n_cmp=n_cmp, n_sel=n_sel),
        grid=(b, t // tq),
        in_specs=[
            pl.BlockSpec((1, tq, 512), lambda bi, i: (bi, i, 0)),
            pl.BlockSpec((1, n_cmp, LANES), lambda bi, i: (bi, 0, 0)),
            pl.BlockSpec((1, n_cmp, LANES), lambda bi, i: (bi, 0, 0)),
            pl.BlockSpec((LANES, n_cmp), lambda bi, i: (0, 0)),
        ],
        out_specs=[
            pl.BlockSpec((1, tq, 512), lambda bi, i: (bi, i, 0)),
            pl.BlockSpec((1, tq, LANES), lambda bi, i: (bi, i, 0)),
            pl.BlockSpec((1, tq // SEL_TILE, 1, LANES), lambda bi, i: (bi, i, 0, 0)),
        ],
        out_shape=[
            jax.ShapeDtypeStruct((b, t, 512), BRANCH_DTYPE),
            jax.ShapeDtypeStruct((b, t, LANES), MXU_DTYPE),
            jax.ShapeDtypeStruct((b, t // SEL_TILE, 1, LANES), jnp.int32),
        ],
        scratch_shapes=[
            pltpu.VMEM((N_GROUPS, N_REP * tq, AUG_K), MXU_DTYPE),
            pltpu.VMEM((N_GROUPS, SEL_BLOCK, tq), F32),
            pltpu.VMEM((N_GROUPS, SEL_BLOCK, tq), jnp.int32),
        ],
        compiler_params=_params("arbitrary", "arbitrary"),
        name="cmp_attn",
    )(p16, kc, vc, ovl)


def _build_kv(k_ref, v_ref, kaug_scr, vaug_scr, *, tile, seq, pad_tiles, select):
    half = _lane_half((tile, LANES))
    lane = lax.broadcasted_iota(jnp.int32, (tile, LANES), 1)
    for c in range(pad_tiles):
        for g in range(N_GROUPS):
            pad = jnp.where((half != g) & ((lane & _BLK_MASK) == 0), -MASK_BIG, 0.0).astype(MXU_DTYPE)
            kaug_scr[g, c * tile:(c + 1) * tile, 0:LANES] = pad
            kaug_scr[g, c * tile:(c + 1) * tile, LANES:AUG_K] = jnp.zeros((tile, LANES), MXU_DTYPE)
            vaug_scr[g, c] = jnp.zeros((LANES, tile), MXU_DTYPE)

    def body(c, carry):
        r0 = pl.multiple_of(c * tile, tile)
        k = k_ref[0, pl.ds(r0, tile), :]
        v = v_ref[0, pl.ds(r0, tile), :].astype(F32)
        pos = r0 + lax.broadcasted_iota(jnp.int32, (tile, LANES), 0)
        k_ali = _k_alibi(pos)
        if select:
            off = jnp.where((lane & _BLK_MASK) == (pos >> _BLK_SHIFT), -MASK_BIG, 0.0).astype(MXU_DTYPE)
        else:
            off = jnp.zeros((tile, LANES), MXU_DTYPE)
        dst = pl.multiple_of(r0 + pad_tiles * tile, tile)
        v_t = jnp.transpose(v)
        sub_half = lax.broadcasted_iota(jnp.int32, (LANES, tile), 0) >> _BLK_SHIFT
        for g in range(N_GROUPS):
            kaug_scr[g, pl.ds(dst, tile), 0:LANES] = jnp.where(half == g, k, off)
            kaug_scr[g, pl.ds(dst, tile), LANES:AUG_K] = k_ali
            vaug_scr[g, c + pad_tiles] = jnp.where(sub_half == g, v_t, 1.0).astype(MXU_DTYPE)
        return carry
    lax.fori_loop(0, seq // tile, body, 0)


def _mask_bias(tile, off, window):
    rows = N_REP * tile
    key = lax.broadcasted_iota(jnp.int32, (tile, rows), 0)
    qry = lax.broadcasted_iota(jnp.int32, (tile, rows), 1) & (tile - 1)
    dist = qry - key + off
    keep = dist >= 0
    if window is not None:
        keep = keep & (dist < window)
    return jnp.where(keep, 0.0, NEG_INF)


def _emit_slabs(o_ref, row0, per_group, tile):
    sub = lax.broadcasted_iota(jnp.int32, (LANES, N_REP * tile), 0)
    both = jnp.where(sub < HEAD_DIM, per_group[0], per_group[1])
    for r in range(N_REP):
        o_ref[0, row0:row0 + tile, r * LANES:(r + 1) * LANES] = jnp.transpose(
            both[:, r * tile:(r + 1) * tile]).astype(o_ref.dtype)


def _sel_kernel(need_ref, q_ref, ns_ref, k_ref, v_ref, o_ref, kaug_scr, vaug_scr, qaug_scr, s_a, s_b, t_a, t_b, bias_scr,
                acc_scr, m_scr, todo_scr, *, tile, seq):
    i = pl.program_id(1)
    t0 = i * tile
    n_q = seq // tile

    base = (pl.program_id(0) * n_q + i) * n_q

    def scan(j, n):
        wanted = need_ref[base + j] > 0

        @pl.when(wanted)
        def _():
            todo_scr[n] = j
        return n + wanted.astype(jnp.int32)
    n_todo = lax.fori_loop(0, i, scan, 0)
    todo_scr[n_todo] = i

    @pl.when(i == 0)
    def _():
        _build_kv(k_ref, v_ref, kaug_scr, vaug_scr, tile=tile, seq=seq, pad_tiles=0, select=True)
        bias_scr[...] = _mask_bias(tile, 0, None)

    _build_qaug(qaug_scr, lambda r: q_ref[0, :, r * LANES:(r + 1) * LANES], t0, tile, ns_ref[0])
    acc_scr[...] = jnp.zeros_like(acc_scr)
    m_scr[...] = jnp.full_like(m_scr, -3e38)

    s_a, s_b = (s_a, t_a), (s_b, t_b)

    def scores_into(buf, j):
        s_buf, top_buf = buf
        k0 = pl.multiple_of(j * tile, tile)
        for g in range(N_GROUPS):
            s = _dot_nt(kaug_scr[g, pl.ds(k0, tile), :], qaug_scr[g])
            s_buf[g] = s
            top_buf[g] = jnp.max(s, axis=0, keepdims=True)

    def consume(buf, j, diagonal):
        s_buf, top_buf = buf
        for g in range(N_GROUPS):
            s = s_buf[g]
            if diagonal:
                s = s + bias_scr[...]
                top = jnp.max(s, axis=0, keepdims=True)
            else:
                top = top_buf[g]
            m_old = m_scr[g]
            m_new = jnp.maximum(m_old, top)
            p = jnp.exp(s - m_new).astype(MXU_DTYPE)
            acc_scr[g] = acc_scr[g] * jnp.exp(m_old - m_new) + _dot(vaug_scr[g, j], p)
            m_scr[g] = m_new

    scores_into(s_a, todo_scr[0])

    def pair(jj, carry):
        first, second = todo_scr[2 * jj], todo_scr[2 * jj + 1]
        scores_into(s_b, second)
        consume(s_a, first, False)
        scores_into(s_a, todo_scr[2 * jj + 2])
        consume(s_b, second, False)
        return carry
    lax.fori_loop(0, n_todo >> 1, pair, 0)

    @pl.when((n_todo & 1) == 0)
    def _():
        consume(s_a, i, True)

    @pl.when((n_todo & 1) == 1)
    def _():
        scores_into(s_b, i)
        consume(s_a, todo_scr[n_todo - 1], False)
        consume(s_b, i, True)

    outs = []
    for g in range(N_GROUPS):
        acc = acc_scr[g]
        den_row = (1 - g) * HEAD_DIM
        outs.append(acc / acc[den_row:den_row + 1, :])
    _emit_slabs(o_ref, 0, outs, tile)


def _sel_attention(p16, ns, block_need, q_blk, k_blk, v_blk, *, tile):
    b, t, _ = p16.shape
    tile = min(tile, t)
    n_q = t // tile
    rows = N_REP * tile
    assert block_need.shape[1] == n_q
    per_block = jnp.maximum(block_need[:, :, 0, 0:SEL_BLOCK], block_need[:, :, 0, SEL_BLOCK:])[:, :, 0:t // SEL_BLOCK]
    tile_need = per_block.reshape(b, n_q, n_q, tile // SEL_BLOCK).max(axis=-1).reshape(-1)
    grid_spec = pltpu.PrefetchScalarGridSpec(
        num_scalar_prefetch=1,
        grid=(b, n_q),
        in_specs=[
            pl.BlockSpec((1, tile, 512), lambda bi, i, need: (bi, i, q_blk)),
            pl.BlockSpec((1, tile, LANES), lambda bi, i, need: (bi, i, 0)),
            pl.BlockSpec((1, t, LANES), lambda bi, i, need: (bi, 0, k_blk)),
            pl.BlockSpec((1, t, LANES), lambda bi, i, need: (bi, 0, v_blk)),
        ],
        out_specs=pl.BlockSpec((1, tile, 512), lambda bi, i, need: (bi, i, 0)),
        scratch_shapes=[
            pltpu.VMEM((N_GROUPS, t, AUG_K), MXU_DTYPE),
            pltpu.VMEM((N_GROUPS, t // tile, LANES, tile), MXU_DTYPE),
            pltpu.VMEM((N_GROUPS, rows, AUG_K), MXU_DTYPE),
            pltpu.VMEM((N_GROUPS, tile, rows), F32),
            pltpu.VMEM((N_GROUPS, tile, rows), F32),
            pltpu.VMEM((N_GROUPS, 1, rows), F32),
            pltpu.VMEM((N_GROUPS, 1, rows), F32),
            pltpu.VMEM((tile, rows), F32),
            pltpu.VMEM((N_GROUPS, LANES, rows), F32),
            pltpu.VMEM((N_GROUPS, 1, rows), F32),
            pltpu.SMEM((n_q + 1,), jnp.int32),
        ],
    )
    return pl.pallas_call(
        functools.partial(_sel_kernel, tile=tile, seq=t),
        grid_spec=grid_spec,
        out_shape=jax.ShapeDtypeStruct((b, t, 512), BRANCH_DTYPE),
        compiler_params=_params("arbitrary", "arbitrary"),
        name="attn_sel",
    )(tile_need, p16, ns, p16, p16)


def _band_kernel(*refs, tile, seq, n_prev, window, has_sink):
    refs = list(refs)
    sink_ref = refs.pop(0) if has_sink else None
    q_ref, qn_ref, k_ref, v_ref, o_ref, kaug_scr, vaug_scr, qaug_a, qaug_b, s_a, s_b, m_a, m_b, bias_scr = refs
    i2 = pl.program_id(1)
    n_q = seq // tile
    span = n_prev + 1
    rows = N_REP * tile
    lane = lax.broadcasted_iota(jnp.int32, (tile, LANES), 1)
    pad_flag = jnp.where((lane & _BLK_MASK) == 0, 1.0, 0.0).astype(MXU_DTYPE)
    offs = [(n_prev - n) * tile for n in range(span)]
    masked = [n for n in range(span) if offs[n] - (tile - 1) < 0 or offs[n] + tile - 1 >= window]

    def scores_into(s_buf, m_buf, qaug_buf, qt):
        k0 = pl.multiple_of(qt * tile, tile)
        for g in range(N_GROUPS):
            s = _dot_nt(kaug_scr[g, pl.ds(k0, span * tile), :], qaug_buf[g])
            m = None
            for n in range(span):
                part = s[n * tile:(n + 1) * tile]
                part = part + bias_scr[masked.index(n)] if n in masked else part
                s_buf[g, n * tile:(n + 1) * tile, :] = part
                top = jnp.max(part, axis=0, keepdims=True)
                m = top if m is None else jnp.maximum(m, top)
            m_buf[g] = m

    def finish(s_buf, m_buf, qt, row0):
        outs = []
        for g in range(N_GROUPS):
            def piece(n):
                return s_buf[g, n * tile:(n + 1) * tile, :]
            m = m_buf[g]
            pv = None
            for n in range(span):
                p = jnp.exp(piece(n) - m).astype(MXU_DTYPE)
                d = _dot(vaug_scr[g, qt + n], p)
                pv = d if pv is None else pv + d
            den_row = (1 - g) * HEAD_DIM
            den = pv[den_row:den_row + 1, :]
            if has_sink:
                col = lax.broadcasted_iota(jnp.int32, (1, rows), 1)
                sink = jnp.zeros((1, rows), F32)
                for r in range(N_REP):
                    sink = jnp.where(col >= r * tile, sink_ref[g * N_REP + r], sink)
                m2 = jnp.maximum(m, sink)
                scale = jnp.exp(m - m2)
                pv = pv * scale
                den = den * scale + jnp.exp(sink - m2)
            outs.append(pv / den)
        _emit_slabs(o_ref, row0, outs, tile)

    @pl.when(i2 == 0)
    def _():
        _build_kv(k_ref, v_ref, kaug_scr, vaug_scr, tile=tile, seq=seq, pad_tiles=n_prev, select=False)
        for idx, n in enumerate(masked):
            bias_scr[idx] = _mask_bias(tile, offs[n], window)
        _build_qaug(qaug_a, lambda r: q_ref[0, 0:tile, r * LANES:(r + 1) * LANES], 0, tile, pad_flag)
        scores_into(s_a, m_a, qaug_a, 0)

    qt = 2 * i2
    _build_qaug(qaug_b, lambda r: q_ref[0, tile:2 * tile, r * LANES:(r + 1) * LANES], (qt + 1) * tile, tile, pad_flag)
    finish(s_a, m_a, qt, 0)
    scores_into(s_b, m_b, qaug_b, qt + 1)
    qn = jnp.minimum(qt + 2, n_q - 1)
    _build_qaug(qaug_a, lambda r: qn_ref[0, :, r * LANES:(r + 1) * LANES], qn * tile, tile, pad_flag)
    scores_into(s_a, m_a, qaug_a, qn)
    finish(s_b, m_b, qt + 1, tile)


def _band_attention(p16, q_blk, k_blk, v_blk, *, tile, window, sinks=None):
    b, t, _ = p16.shape
    tile = min(tile, t // 2)
    n_q = t // tile
    n_prev = -(-(window - 1) // tile)
    span = n_prev + 1
    assert n_q % 2 == 0
    has_sink = sinks is not None
    rows = N_REP * tile
    offs = [(n_prev - n) * tile for n in range(span)]
    n_masked = sum(1 for o in offs if o - (tile - 1) < 0 or o + tile - 1 >= window)
    in_specs, args = [], []
    if has_sink:
        in_specs.append(pl.BlockSpec(memory_space=pltpu.SMEM))
        args.append(sinks.astype(F32))
    in_specs += [
        pl.BlockSpec((1, 2 * tile, 512), lambda bi, i: (bi, i, q_blk)),
        pl.BlockSpec((1, tile, 512), lambda bi, i: (bi, jnp.minimum(2 * i + 2, n_q - 1), q_blk)),
        pl.BlockSpec((1, t, LANES), lambda bi, i: (bi, 0, k_blk)),
        pl.BlockSpec((1, t, LANES), lambda bi, i: (bi, 0, v_blk)),
    ]
    args += [p16, p16, p16, p16]
    return pl.pallas_call(
        functools.partial(_band_kernel, tile=tile, seq=t, n_prev=n_prev, window=window, has_sink=has_sink),
        grid=(b, n_q // 2),
        in_specs=in_specs,
        out_specs=pl.BlockSpec((1, 2 * tile, 512), lambda bi, i: (bi, i, 0)),
        out_shape=jax.ShapeDtypeStruct((b, t, 512), BRANCH_DTYPE),
        scratch_shapes=[
            pltpu.VMEM((N_GROUPS, t + n_prev * tile, AUG_K), MXU_DTYPE),
            pltpu.VMEM((N_GROUPS, n_q + n_prev, LANES, tile), MXU_DTYPE),
            pltpu.VMEM((N_GROUPS, rows, AUG_K), MXU_DTYPE),
            pltpu.VMEM((N_GROUPS, rows, AUG_K), MXU_DTYPE),
            pltpu.VMEM((N_GROUPS, span * tile, rows), F32),
            pltpu.VMEM((N_GROUPS, span * tile, rows), F32),
            pltpu.VMEM((N_GROUPS, 1, rows), F32),
            pltpu.VMEM((N_GROUPS, 1, rows), F32),
            pltpu.VMEM((n_masked, tile, rows), F32),
        ],
        compiler_params=_params("arbitrary", "arbitrary"),
        name="attn_band" + ("_sink" if has_sink else ""),
    )(*args)


def _slab_cols(w):
    lead = w.shape[:-1]
    return w.reshape(*lead, N_GROUPS, N_REP, HEAD_DIM).swapaxes(-3, -2).reshape(*lead, N_GROUPS * N_REP * HEAD_DIM)


def _arrange_proj_weight(w):
    q_a = _slab_cols(w[:, 0:512])
    kc, vc, ks, vs, kw, vw = [w[:, 512 + j * LANES:512 + (j + 1) * LANES] for j in range(6)]
    gates = w[:, 1280:1304].reshape(-1, N_GROUPS, N_REP, N_BRANCH).transpose(0, 3, 2, 1).reshape(-1, 24)
    gates = jnp.pad(gates, ((0, 0), (0, LANES - 24)))
    q_b = _slab_cols(w[:, 1304:1816])
    k_b, v_b = w[:, 1816:1944], w[:, 1944:2072]
    return jnp.concatenate([q_a, q_b, ks, vs, kw, vw, k_b, v_b, kc, vc, gates], axis=1).astype(MXU_DTYPE)


def _arrange_out_weight(w):
    w_a = _slab_cols(w[0:512].T).T
    w_b = _slab_cols(w[512:1024].T).T
    return jnp.concatenate([w_a, w_b], axis=0).astype(MXU_DTYPE)


_QA_BLK, _QB_BLK = 0, 1
_KS_BLK, _VS_BLK, _KW_BLK, _VW_BLK, _KB_BLK, _VB_BLK = 8, 9, 10, 11, 12, 13


def kernel(x, ffn1_norm, ffn1_w_in, ffn1_w_out, mix_norm, w_mix_in, cmp_k_pos, cmp_k_w1, cmp_k_b1, cmp_k_w2, cmp_v_pos, cmp_v_w1, cmp_v_b1, cmp_v_w2, swa_sinks, w_mix_out, ffn2_norm, ffn2_w_in, ffn2_w_out, final_norm):
    b, t, d = x.shape
    m = b * t
    depth = ffn1_norm.shape[0]
    h = x.reshape(m, d)
    for l in range(depth):
        last = l == depth - 1
        h, p16, kc_raw, vc_raw, gates = _ffn_proj(
            h, ffn1_norm[l], ffn1_w_in[l].astype(MXU_DTYPE), ffn1_w_out[l].astype(MXU_DTYPE),
            mix_norm[l], _arrange_proj_weight(w_mix_in[l]))
        p16 = p16.reshape(b, t, P16_W)
        kc = _compress(kc_raw.reshape(b, t, LANES), cmp_k_pos[l], cmp_k_w1[l], cmp_k_b1[l], cmp_k_w2[l])
        vc = _compress(vc_raw.reshape(b, t, LANES), cmp_v_pos[l], cmp_v_w1[l], cmp_v_b1[l], cmp_v_w2[l])
        o_cmp, notsel, block_need = _cmp_attention(p16, kc, vc)
        o_slc = _sel_attention(p16, notsel, block_need, _QA_BLK, _KS_BLK, _VS_BLK, tile=SEL_TILE)
        o_win = _band_attention(p16, _QA_BLK, _KW_BLK, _VW_BLK, tile=256, window=NSA_WINDOW)
        o_b = _band_attention(p16, _QB_BLK, _KB_BLK, _VB_BLK, tile=128, window=SWA_WINDOW, sinks=swa_sinks[l])
        h = _combine_ffn(
            h, o_cmp.reshape(m, 512), o_slc.reshape(m, 512), o_win.reshape(m, 512), o_b.reshape(m, 512), gates,
            _arrange_out_weight(w_mix_out[l]), ffn2_norm[l], ffn2_w_in[l].astype(MXU_DTYPE),
            ffn2_w_out[l].astype(MXU_DTYPE), final_norm if last else None)
    if depth == 0:
        raise ValueError("depth must be positive")
    return h.reshape(b, t, d)
```

```python
import functools

import numpy as np
import jax
import jax.numpy as jnp
from jax import lax
from jax.experimental import pallas as pl
from jax.experimental.pallas import tpu as pltpu

F32 = jnp.float32
MXU_DTYPE = jnp.bfloat16
BRANCH_DTYPE = jnp.bfloat16

D_MODEL = 1024
HEAD_DIM = 64
N_GROUPS = 2
N_REP = 4
CMP_BLOCK = 32
CMP_STRIDE = 16
CMP_HIDDEN = 256
SEL_BLOCK = 64
SEL_TOPK = 16
NSA_WINDOW = 512
SWA_WINDOW = 128
SEL_TILE = 256
D_FF = 2816
NORM_EPS = 1e-6
NEG_INF = -1e30
FORCE_SCORE = 1e9
ATTN_SCALE = HEAD_DIM ** -0.5
MASK_BIG = 2.0 ** 100
LANES = 128
AUG_K = 2 * LANES
_BLK_SHIFT = 6
_BLK_MASK = SEL_BLOCK - 1
VMEM_LIMIT = 48 * 1024 * 1024

_SLOPES = (2.0 ** (-8.0 * np.arange(1, 9) / 8)).reshape(N_GROUPS, N_REP)

_NT = (((1,), (1,)), ((), ()))


def _dot(a, b):
    return jnp.dot(a, b, preferred_element_type=F32)


def _dot_nt(a, b):
    return lax.dot_general(a, b, _NT, preferred_element_type=F32)


def _rmsnorm(x, g):
    ms = jnp.mean(x * x, axis=-1, keepdims=True)
    return x * lax.rsqrt(ms + NORM_EPS) * g


def _params(*sem):
    return pltpu.CompilerParams(dimension_semantics=sem, vmem_limit_bytes=VMEM_LIMIT)


FFN_CHUNK = 256
FFN_ROWS = 512


def _swiglu_half_step(x, g_ref, wi_ref, wo_ref):
    xn = _rmsnorm(x, g_ref[...]).astype(MXU_DTYPE)
    acc = None
    for f in range(D_FF // FFN_CHUNK):
        cols = slice(f * FFN_CHUNK, (f + 1) * FFN_CHUNK)
        gate = _dot(xn, wi_ref[:, cols])
        up = _dot(xn, wi_ref[:, D_FF + f * FFN_CHUNK:D_FF + (f + 1) * FFN_CHUNK])
        act = gate * jax.nn.sigmoid(gate) * up
        part = _dot(act.astype(MXU_DTYPE), wo_ref[cols, :])
        acc = part if acc is None else acc + part
    return x + 0.5 * acc


P16_W = 1024 + 3 * LANES
PROJ_W = P16_W + 3 * LANES
VT_ROWS = 3 * LANES


def _ffn_proj_kernel(x_ref, g_ref, wi_ref, wo_ref, pg_ref, pw_ref, pvt_ref, h_ref, p16_ref, kc_ref, vc_ref, gt_ref,
                     vt_ref):
    h = _swiglu_half_step(x_ref[...], g_ref, wi_ref, wo_ref)
    h_ref[...] = h
    hn = _rmsnorm(h, pg_ref[...]).astype(MXU_DTYPE)
    vt_ref[...] = _dot_nt(pvt_ref[...], hn).astype(MXU_DTYPE)
    y = _dot(hn, pw_ref[...])
    p16_ref[:, 0:1024] = (y[:, 0:1024] * ATTN_SCALE).astype(MXU_DTYPE)
    p16_ref[:, 1024:P16_W] = y[:, 1024:P16_W].astype(MXU_DTYPE)
    kc_ref[...] = y[:, P16_W:P16_W + LANES]
    vc_ref[...] = y[:, P16_W + LANES:P16_W + 2 * LANES]
    gt_ref[...] = y[:, P16_W + 2 * LANES:PROJ_W]


def _resident(shape):
    return pl.BlockSpec(shape, lambda i: (0,) * len(shape), pipeline_mode=pl.Buffered(1))


def _ffn_proj(x2, norm_g, w_in, w_out, proj_g, proj_w, proj_wvt):
    m = x2.shape[0]
    tm = min(FFN_ROWS, m)
    row = lambda i: (i, 0)
    return pl.pallas_call(
        _ffn_proj_kernel,
        grid=(m // tm,),
        in_specs=[
            pl.BlockSpec((tm, D_MODEL), row),
            _resident((1, D_MODEL)),
            _resident((D_MODEL, 2 * D_FF)),
            _resident((D_FF, D_MODEL)),
            _resident((1, D_MODEL)),
            _resident((D_MODEL, PROJ_W)),
            _resident((VT_ROWS, D_MODEL)),
        ],
        out_specs=[
            pl.BlockSpec((tm, D_MODEL), row),
            pl.BlockSpec((tm, P16_W), row),
            pl.BlockSpec((tm, LANES), row),
            pl.BlockSpec((tm, LANES), row),
            pl.BlockSpec((tm, LANES), row),
            pl.BlockSpec((VT_ROWS, tm), lambda i: (0, i)),
        ],
        out_shape=[
            jax.ShapeDtypeStruct((m, D_MODEL), F32),
            jax.ShapeDtypeStruct((m, P16_W), MXU_DTYPE),
            jax.ShapeDtypeStruct((m, LANES), F32),
            jax.ShapeDtypeStruct((m, LANES), F32),
            jax.ShapeDtypeStruct((m, LANES), F32),
            jax.ShapeDtypeStruct((VT_ROWS, m), MXU_DTYPE),
        ],
        compiler_params=_params("arbitrary"),
        name="ffn_proj",
    )(x2, norm_g.reshape(1, D_MODEL), w_in, w_out, proj_g.reshape(1, D_MODEL), proj_w, proj_wvt)


N_BRANCH = 3
GATE_EXP_W = N_BRANCH * N_REP * LANES


def _combine_ffn_kernel(h_ref, oc_ref, os_ref, ow_ref, ob_ref, gt_ref, e_ref, w_ref, g_ref, wi_ref, wo_ref, *rest,
                        final_norm):
    if final_norm:
        fg_ref, o_ref = rest
    else:
        (o_ref,) = rest
    sig = jax.nn.sigmoid(gt_ref[...])
    hi = sig.astype(MXU_DTYPE)
    lo = (sig - hi.astype(F32)).astype(MXU_DTYPE)
    gx = _dot(hi, e_ref[...]) + _dot(lo, e_ref[...])
    acc = h_ref[...]
    branches = (oc_ref, os_ref, ow_ref)
    for r in range(N_REP):
        sl = slice(r * LANES, (r + 1) * LANES)
        o_a = None
        for br in range(N_BRANCH):
            c = (br * N_REP + r) * LANES
            term = gx[:, c:c + LANES] * branches[br][:, sl]
            o_a = term if o_a is None else o_a + term
        acc = acc + _dot(o_a.astype(MXU_DTYPE), w_ref[sl, :])
    acc = acc + _dot(ob_ref[...].astype(MXU_DTYPE), w_ref[512:1024, :])
    h = _swiglu_half_step(acc, g_ref, wi_ref, wo_ref)
    if final_norm:
        h = _rmsnorm(h, fg_ref[...])
    o_ref[...] = h


def _gate_expand():
    e = np.zeros((LANES, GATE_EXP_W), np.float32)
    for br in range(N_BRANCH):
        for r in range(N_REP):
            for g in range(N_GROUPS):
                c0 = (br * N_REP + r) * LANES + g * HEAD_DIM
                e[br * 8 + r * 2 + g, c0:c0 + HEAD_DIM] = 1.0
    return e


def _combine_ffn(h2, o_cmp, o_slc, o_win, o_b, gates, w_out_p, norm_g, w_in, w_out, final_g=None):
    m = h2.shape[0]
    tm = min(FFN_ROWS, m)
    row = lambda i: (i, 0)
    final_norm = final_g is not None
    e = jnp.asarray(_gate_expand(), MXU_DTYPE)
    in_specs = [
        pl.BlockSpec((tm, D_MODEL), row),
        pl.BlockSpec((tm, 512), row),
        pl.BlockSpec((tm, 512), row),
        pl.BlockSpec((tm, 512), row),
        pl.BlockSpec((tm, 512), row),
        pl.BlockSpec((tm, LANES), row),
        _resident((LANES, GATE_EXP_W)),
        _resident((D_MODEL, D_MODEL)),
        _resident((1, D_MODEL)),
        _resident((D_MODEL, 2 * D_FF)),
        _resident((D_FF, D_MODEL)),
    ]
    args = [h2, o_cmp, o_slc, o_win, o_b, gates, e, w_out_p, norm_g.reshape(1, D_MODEL), w_in, w_out]
    if final_norm:
        in_specs.append(_resident((1, D_MODEL)))
        args.append(final_g.reshape(1, D_MODEL))
    return pl.pallas_call(
        functools.partial(_combine_ffn_kernel, final_norm=final_norm),
        grid=(m // tm,),
        in_specs=in_specs,
        out_specs=pl.BlockSpec((tm, D_MODEL), row),
        out_shape=jax.ShapeDtypeStruct((m, D_MODEL), F32),
        compiler_params=_params("arbitrary"),
        name="combine_ffn_final" if final_norm else "combine_ffn",
    )(*args)


def _gelu_tanh(x):
    return 0.5 * x * (1.0 + jnp.tanh(np.sqrt(2.0 / np.pi) * (x + 0.044715 * (x * x * x))))


def _compress_kernel(raw_ref, pa_ref, pb_ref, wa_ref, wb_ref, b1_ref, w2_ref, o_ref):
    x = raw_ref[0]
    n_rows = x.shape[0]
    first = _dot((x + pa_ref[...]).astype(MXU_DTYPE), wa_ref[...])
    second = _dot((x + pb_ref[...]).astype(MXU_DTYPE), wb_ref[...])
    hid = first + pltpu.roll(second, n_rows - 1, axis=0) + b1_ref[...]
    out = _dot(_gelu_tanh(hid).astype(MXU_DTYPE), w2_ref[...])
    row = lax.broadcasted_iota(jnp.int32, out.shape, 0)
    o_ref[0] = jnp.where(row == n_rows - 1, 0.0, out).astype(o_ref.dtype)


def _compress(raw, pos, w1, b1, w2):
    b, t, _ = raw.shape
    n_rows = t // CMP_STRIDE
    half = CMP_STRIDE
    kw = half * LANES
    eye = jnp.eye(N_GROUPS, dtype=F32)
    w1r = w1.reshape(CMP_BLOCK, HEAD_DIM, CMP_HIDDEN)

    def expand(wpart):
        return jnp.einsum("jdh,ab->jadbh", wpart, eye).reshape(kw, N_GROUPS * CMP_HIDDEN).astype(MXU_DTYPE)

    def expand_pos(ppart):
        return jnp.broadcast_to(ppart[:, None, :], (half, N_GROUPS, HEAD_DIM)).reshape(1, kw)

    wa, wb = expand(w1r[:half]), expand(w1r[half:])
    pa, pb = expand_pos(pos[:half]), expand_pos(pos[half:])
    b1e = jnp.tile(b1.reshape(1, CMP_HIDDEN), (1, N_GROUPS))
    w2e = jnp.einsum("hd,ab->ahbd", w2, eye).reshape(N_GROUPS * CMP_HIDDEN, LANES).astype(MXU_DTYPE)
    const = lambda i: (0, 0)
    return pl.pallas_call(
        _compress_kernel,
        grid=(b,),
        in_specs=[
            pl.BlockSpec((1, n_rows, kw), lambda i: (i, 0, 0)),
            pl.BlockSpec((1, kw), const),
            pl.BlockSpec((1, kw), const),
            pl.BlockSpec((kw, N_GROUPS * CMP_HIDDEN), const),
            pl.BlockSpec((kw, N_GROUPS * CMP_HIDDEN), const),
            pl.BlockSpec((1, N_GROUPS * CMP_HIDDEN), const),
            pl.BlockSpec((N_GROUPS * CMP_HIDDEN, LANES), const),
        ],
        out_specs=pl.BlockSpec((1, n_rows, LANES), lambda i: (i, 0, 0)),
        out_shape=jax.ShapeDtypeStruct((b, n_rows, LANES), MXU_DTYPE),
        compiler_params=_params("arbitrary"),
        name="compress",
    )(raw.reshape(b, n_rows, kw), pa, pb, wa, wb, b1e, w2e)


def _lane_half(shape):
    return lax.broadcasted_iota(jnp.int32, shape, len(shape) - 1) >> _BLK_SHIFT


def _alibi_cols(c0, c1, c2, c3, shape):
    lane = lax.broadcasted_iota(jnp.int32, shape, 1)
    z = jnp.zeros(shape, F32)
    return jnp.where(lane == 0, c0, jnp.where(lane == 1, c1, jnp.where(lane == 2, c2, jnp.where(lane == 3, c3, z))))


def _q_alibi(tq, t0, slope):
    t = t0 + lax.broadcasted_iota(jnp.int32, (tq, LANES), 0)
    hi = (t >> _BLK_SHIFT).astype(F32)
    lo = (t & _BLK_MASK).astype(F32)
    return _alibi_cols(slope * SEL_BLOCK * hi, slope * lo, slope * SEL_BLOCK, slope, (tq, LANES)).astype(MXU_DTYPE)


def _k_alibi(pos):
    hi = (pos >> _BLK_SHIFT).astype(F32)
    lo = (pos & _BLK_MASK).astype(F32)
    return _alibi_cols(-1.0, -1.0, hi, lo, pos.shape).astype(MXU_DTYPE)


def _build_qaug(qaug_scr, slab_of, t0, tq, other):
    half = _lane_half((tq, LANES))
    unit = _q_alibi(tq, t0, 1.0)
    for r in range(N_REP):
        slab = slab_of(r)
        for g in range(N_GROUPS):
            fill = jnp.zeros_like(slab) if other is None else other
            qaug_scr[g, r * tq:(r + 1) * tq, 0:LANES] = jnp.where(half == g, slab, fill)
            qaug_scr[g, r * tq:(r + 1) * tq, LANES:AUG_K] = unit * jnp.asarray(_SLOPES[g, r], MXU_DTYPE)


def _cmp_kernel(q_ref, kc_ref, vc_ref, ovl_ref, o_ref, ns_ref, need_ref, qaug_scr, score_scr, cnt_scr, *, tq, n_cmp, n_sel):
    i = pl.program_id(1)
    t0 = i * tq
    _build_qaug(qaug_scr, lambda r: q_ref[0, :, r * LANES:(r + 1) * LANES], t0, tq, None)

    kc = kc_ref[0]
    vc = vc_ref[0]
    half_k = _lane_half((n_cmp, LANES))
    c_pos = lax.broadcasted_iota(jnp.int32, (n_cmp, LANES), 0) * CMP_STRIDE + (CMP_BLOCK - 1)
    k_ali = _k_alibi(c_pos)
    rows = N_REP * tq
    c_end = lax.broadcasted_iota(jnp.int32, (n_cmp, rows), 0) * CMP_STRIDE + (CMP_BLOCK - 1)
    t_col = t0 + (lax.broadcasted_iota(jnp.int32, (n_cmp, rows), 1) & (tq - 1))
    vis = c_end <= t_col
    vc_t = jnp.transpose(vc.astype(F32)).astype(MXU_DTYPE)

    outs, psl_t = [], []
    for g in range(N_GROUPS):
        kaug = jnp.concatenate([jnp.where(half_k == g, kc, jnp.zeros_like(kc)), k_ali], axis=1)
        s = jnp.where(vis, _dot_nt(kaug, qaug_scr[g]), NEG_INF)
        m = jnp.max(s, axis=0, keepdims=True)
        e = jnp.where(vis, jnp.exp(s - m), 0.0)
        l = jnp.sum(e, axis=0, keepdims=True)
        pb = (e * jnp.where(l > 0.0, 1.0 / l, 0.0)).astype(MXU_DTYPE)
        outs.append(_dot(vc_t, pb))
        pt = _dot(ovl_ref[...], pb)
        psl_t.append(pt[:, 0:tq] + pt[:, tq:2 * tq] + pt[:, 2 * tq:3 * tq] + pt[:, 3 * tq:4 * tq])
    _emit_slabs(o_ref, 0, outs, tq)

    blk = lax.broadcasted_iota(jnp.int32, (SEL_BLOCK, tq), 0)
    cur = (t0 + lax.broadcasted_iota(jnp.int32, (SEL_BLOCK, tq), 1)) >> _BLK_SHIFT
    valid = blk <= cur
    forced = (blk == 0) | (blk == cur) | (blk == cur - 1)
    for n, g in enumerate((1, 0)):
        score_scr[n] = jnp.where(forced, FORCE_SCORE, jnp.where(valid, psl_t[g][0:SEL_BLOCK], NEG_INF))
    cnt_scr[...] = jnp.zeros_like(cnt_scr)

    n_valid = jnp.minimum(((t0 + tq - 1) >> _BLK_SHIFT) + 1, n_sel)
    chunk = 8
    for c0 in range(0, n_sel, chunk):
        @pl.when((n_valid > SEL_TOPK) & (c0 < n_valid))
        def _(c0=c0):
            for n in range(N_GROUPS):
                score = score_scr[n]
                cnt = cnt_scr[n]
                for j in range(c0, min(c0 + chunk, n_sel)):
                    xj = jnp.broadcast_to(score[j:j + 1, :], (SEL_BLOCK, tq))
                    beats = (xj > score) | ((xj == score) & (blk > j))
                    cnt = cnt + jnp.where(beats, 1, 0)
                cnt_scr[n] = cnt

    notsel = [jnp.where((cnt_scr[n] < SEL_TOPK) & valid, 0.0, 1.0) for n in range(N_GROUPS)]
    ns_t = jnp.transpose(jnp.concatenate(notsel, axis=0))
    ns_ref[0] = ns_t.astype(ns_ref.dtype)
    for u in range(tq // SEL_TILE):
        part = ns_t[u * SEL_TILE:(u + 1) * SEL_TILE]
        need_ref[0, u] = (1.0 - jnp.min(part, axis=0, keepdims=True)).astype(jnp.int32)


def _overlap_t(t):
    nc_pad = t // CMP_STRIDE
    ns = t // SEL_BLOCK
    c_start = np.arange(nc_pad) * CMP_STRIDE
    s_start = np.arange(ns) * SEL_BLOCK
    ov = np.clip(np.minimum(c_start[:, None] + CMP_BLOCK, s_start[None, :] + SEL_BLOCK)
                 - np.maximum(c_start[:, None], s_start[None, :]), 0, None) / CMP_BLOCK
    ov[nc_pad - 1] = 0.0
    out = np.zeros((LANES, nc_pad), np.float32)
    out[0:ns] = ov.T
    out[HEAD_DIM:HEAD_DIM + ns] = ov.T
    return out


def _cmp_attention(p16, kc, vc, *, tq=512):
    b, t, _ = p16.shape
    tq = min(tq, t)
    assert tq % SEL_TILE == 0
    n_cmp = t // CMP_STRIDE
    n_sel = t // SEL_BLOCK
    assert n_sel <= SEL_BLOCK and n_cmp % LANES == 0
    ovl = jnp.asarray(_overlap_t(t), MXU_DTYPE)
    return pl.pallas_call(
        functools.partial(_cmp_kernel, tq=tq, n_cmp=n_cmp, n_sel=n_sel),
        grid=(b, t // tq),
        in_specs=[
            pl.BlockSpec((1, tq, 512), lambda bi, i: (bi, i, 0)),
            pl.BlockSpec((1, n_cmp, LANES), lambda bi, i: (bi, 0, 0)),
            pl.BlockSpec((1, n_cmp, LANES), lambda bi, i: (bi, 0, 0)),
            pl.BlockSpec((LANES, n_cmp), lambda bi, i: (0, 0)),
        ],
        out_specs=[
            pl.BlockSpec((1, tq, 512), lambda bi, i: (bi, i, 0)),
            pl.BlockSpec((1, tq, LANES), lambda bi, i: (bi, i, 0)),
            pl.BlockSpec((1, tq // SEL_TILE, 1, LANES), lambda bi, i: (bi, i, 0, 0)),
        ],
        out_shape=[
            jax.ShapeDtypeStruct((b, t, 512), BRANCH_DTYPE),
            jax.ShapeDtypeStruct((b, t, LANES), MXU_DTYPE),
            jax.ShapeDtypeStruct((b, t // SEL_TILE, 1, LANES), jnp.int32),
        ],
        scratch_shapes=[
            pltpu.VMEM((N_GROUPS, N_REP * tq, AUG_K), MXU_DTYPE),
            pltpu.VMEM((N_GROUPS, SEL_BLOCK, tq), F32),
            pltpu.VMEM((N_GROUPS, SEL_BLOCK, tq), jnp.int32),
        ],
        compiler_params=_params("arbitrary", "arbitrary"),
        name="cmp_attn",
    )(p16, kc, vc, ovl)


def _build_kv(k_ref, v_ref, kaug_scr, vaug_scr, *, tile, seq, pad_tiles, select):
    half = _lane_half((tile, LANES))
    lane = lax.broadcasted_iota(jnp.int32, (tile, LANES), 1)
    for c in range(pad_tiles):
        for g in range(N_GROUPS):
            pad = jnp.where((half != g) & ((lane & _BLK_MASK) == 0), -MASK_BIG, 0.0).astype(MXU_DTYPE)
            kaug_scr[g, c * tile:(c + 1) * tile, 0:LANES] = pad
            kaug_scr[g, c * tile:(c + 1) * tile, LANES:AUG_K] = jnp.zeros((tile, LANES), MXU_DTYPE)
            vaug_scr[g, c] = jnp.zeros((LANES, tile), MXU_DTYPE)

    def body(c, carry):
        r0 = pl.multiple_of(c * tile, tile)
        k = k_ref[0, pl.ds(r0, tile), :]
        pos = r0 + lax.broadcasted_iota(jnp.int32, (tile, LANES), 0)
        k_ali = _k_alibi(pos)
        if select:
            off = jnp.where((lane & _BLK_MASK) == (pos >> _BLK_SHIFT), -MASK_BIG, 0.0).astype(MXU_DTYPE)
        else:
            off = jnp.zeros((tile, LANES), MXU_DTYPE)
        dst = pl.multiple_of(r0 + pad_tiles * tile, tile)
        for g in range(N_GROUPS):
            kaug_scr[g, pl.ds(dst, tile), 0:LANES] = jnp.where(half == g, k, off)
            kaug_scr[g, pl.ds(dst, tile), LANES:AUG_K] = k_ali
        return carry
    lax.fori_loop(0, seq // tile, body, 0)

    sub_half = lax.broadcasted_iota(jnp.int32, (LANES, tile), 0) >> _BLK_SHIFT
    for c in range(seq // tile):
        v_t = v_ref[:, c * tile:(c + 1) * tile]
        for g in range(N_GROUPS):
            vaug_scr[g, c + pad_tiles] = jnp.where(sub_half == g, v_t, jnp.ones_like(v_t))


def _mask_bias(tile, off, window):
    rows = N_REP * tile
    key = lax.broadcasted_iota(jnp.int32, (tile, rows), 0)
    qry = lax.broadcasted_iota(jnp.int32, (tile, rows), 1) & (tile - 1)
    dist = qry - key + off
    keep = dist >= 0
    if window is not None:
        keep = keep & (dist < window)
    return jnp.where(keep, 0.0, NEG_INF)


def _emit_slabs(o_ref, row0, per_group, tile):
    sub = lax.broadcasted_iota(jnp.int32, (LANES, N_REP * tile), 0)
    both = jnp.where(sub < HEAD_DIM, per_group[0], per_group[1])
    for r in range(N_REP):
        o_ref[0, row0:row0 + tile, r * LANES:(r + 1) * LANES] = jnp.transpose(
            both[:, r * tile:(r + 1) * tile]).astype(o_ref.dtype)


def _sel_kernel(need_ref, q_ref, ns_ref, k_ref, v_ref, o_ref, kaug_scr, vaug_scr, qaug_scr, s_a, s_b, t_a, t_b, bias_scr,
                acc_scr, m_scr, todo_scr, *, tile, seq):
    i = pl.program_id(1)
    t0 = i * tile
    n_q = seq // tile

    base = (pl.program_id(0) * n_q + i) * n_q

    def scan(j, n):
        wanted = need_ref[base + j] > 0

        @pl.when(wanted)
        def _():
            todo_scr[n] = j
        return n + wanted.astype(jnp.int32)
    n_todo = lax.fori_loop(0, i, scan, 0)
    todo_scr[n_todo] = i

    @pl.when(i == 0)
    def _():
        _build_kv(k_ref, v_ref, kaug_scr, vaug_scr, tile=tile, seq=seq, pad_tiles=0, select=True)
        bias_scr[...] = _mask_bias(tile, 0, None)

    _build_qaug(qaug_scr, lambda r: q_ref[0, :, r * LANES:(r + 1) * LANES], t0, tile, ns_ref[0])
    acc_scr[...] = jnp.zeros_like(acc_scr)
    m_scr[...] = jnp.full_like(m_scr, -3e38)

    s_a, s_b = (s_a, t_a), (s_b, t_b)

    def scores_into(buf, j):
        s_buf, top_buf = buf
        k0 = pl.multiple_of(j * tile, tile)
        for g in range(N_GROUPS):
            s = _dot_nt(kaug_scr[g, pl.ds(k0, tile), :], qaug_scr[g])
            s_buf[g] = s
            top_buf[g] = jnp.max(s, axis=0, keepdims=True)

    def consume(buf, j, diagonal):
        s_buf, top_buf = buf
        for g in range(N_GROUPS):
            s = s_buf[g]
            if diagonal:
                s = s + bias_scr[...]
                top = jnp.max(s, axis=0, keepdims=True)
            else:
                top = top_buf[g]
            m_old = m_scr[g]
            m_new = jnp.maximum(m_old, top)
            p = jnp.exp(s - m_new).astype(MXU_DTYPE)
            acc_scr[g] = acc_scr[g] * jnp.exp(m_old - m_new) + _dot(vaug_scr[g, j], p)
            m_scr[g] = m_new

    scores_into(s_a, todo_scr[0])

    def pair(jj, carry):
        first, second = todo_scr[2 * jj], todo_scr[2 * jj + 1]
        scores_into(s_b, second)
        consume(s_a, first, False)
        scores_into(s_a, todo_scr[2 * jj + 2])
        consume(s_b, second, False)
        return carry
    lax.fori_loop(0, n_todo >> 1, pair, 0)

    @pl.when((n_todo & 1) == 0)
    def _():
        consume(s_a, i, True)

    @pl.when((n_todo & 1) == 1)
    def _():
        scores_into(s_b, i)
        consume(s_a, todo_scr[n_todo - 1], False)
        consume(s_b, i, True)

    outs = []
    for g in range(N_GROUPS):
        acc = acc_scr[g]
        den_row = (1 - g) * HEAD_DIM
        outs.append(acc / acc[den_row:den_row + 1, :])
    _emit_slabs(o_ref, 0, outs, tile)


def _sel_attention(p16, vt, ns, block_need, q_blk, k_blk, v_blk, *, tile):
    b, t, _ = p16.shape
    tile = min(tile, t)
    n_q = t // tile
    rows = N_REP * tile
    assert block_need.shape[1] == n_q
    per_block = jnp.maximum(block_need[:, :, 0, 0:SEL_BLOCK], block_need[:, :, 0, SEL_BLOCK:])[:, :, 0:t // SEL_BLOCK]
    tile_need = per_block.reshape(b, n_q, n_q, tile // SEL_BLOCK).max(axis=-1).reshape(-1)
    grid_spec = pltpu.PrefetchScalarGridSpec(
        num_scalar_prefetch=1,
        grid=(b, n_q),
        in_specs=[
            pl.BlockSpec((1, tile, 512), lambda bi, i, need: (bi, i, q_blk)),
            pl.BlockSpec((1, tile, LANES), lambda bi, i, need: (bi, i, 0)),
            pl.BlockSpec((1, t, LANES), lambda bi, i, need: (bi, 0, k_blk)),
            pl.BlockSpec((LANES, t), lambda bi, i, need: (v_blk, bi)),
        ],
        out_specs=pl.BlockSpec((1, tile, 512), lambda bi, i, need: (bi, i, 0)),
        scratch_shapes=[
            pltpu.VMEM((N_GROUPS, t, AUG_K), MXU_DTYPE),
            pltpu.VMEM((N_GROUPS, t // tile, LANES, tile), MXU_DTYPE),
            pltpu.VMEM((N_GROUPS, rows, AUG_K), MXU_DTYPE),
            pltpu.VMEM((N_GROUPS, tile, rows), F32),
            pltpu.VMEM((N_GROUPS, tile, rows), F32),
            pltpu.VMEM((N_GROUPS, 1, rows), F32),
            pltpu.VMEM((N_GROUPS, 1, rows), F32),
            pltpu.VMEM((tile, rows), F32),
            pltpu.VMEM((N_GROUPS, LANES, rows), F32),
            pltpu.VMEM((N_GROUPS, 1, rows), F32),
            pltpu.SMEM((n_q + 1,), jnp.int32),
        ],
    )
    return pl.pallas_call(
        functools.partial(_sel_kernel, tile=tile, seq=t),
        grid_spec=grid_spec,
        out_shape=jax.ShapeDtypeStruct((b, t, 512), BRANCH_DTYPE),
        compiler_params=_params("arbitrary", "arbitrary"),
        name="attn_sel",
    )(tile_need, p16, ns, p16, vt)


def _band_kernel(*refs, tile, seq, n_prev, window, has_sink):
    refs = list(refs)
    sink_ref = refs.pop(0) if has_sink else None
    q_ref, qn_ref, k_ref, v_ref, o_ref, kaug_scr, vaug_scr, qaug_a, qaug_b, s_a, s_b, m_a, m_b, bias_scr = refs
    i2 = pl.program_id(1)
    n_q = seq // tile
    span = n_prev + 1
    rows = N_REP * tile
    lane = lax.broadcasted_iota(jnp.int32, (tile, LANES), 1)
    pad_flag = jnp.where((lane & _BLK_MASK) == 0, 1.0, 0.0).astype(MXU_DTYPE)
    offs = [(n_prev - n) * tile for n in range(span)]
    masked = [n for n in range(span) if offs[n] - (tile - 1) < 0 or offs[n] + tile - 1 >= window]

    def scores_into(s_buf, m_buf, qaug_buf, qt):
        k0 = pl.multiple_of(qt * tile, tile)
        for g in range(N_GROUPS):
            s = _dot_nt(kaug_scr[g, pl.ds(k0, span * tile), :], qaug_buf[g])
            m = None
            for n in range(span):
                part = s[n * tile:(n + 1) * tile]
                part = part + bias_scr[masked.index(n)] if n in masked else part
                s_buf[g, n * tile:(n + 1) * tile, :] = part
                top = jnp.max(part, axis=0, keepdims=True)
                m = top if m is None else jnp.maximum(m, top)
            m_buf[g] = m

    def finish(s_buf, m_buf, qt, row0):
        outs = []
        for g in range(N_GROUPS):
            def piece(n):
                return s_buf[g, n * tile:(n + 1) * tile, :]
            m = m_buf[g]
            pv = None
            for n in range(span):
                p = jnp.exp(piece(n) - m).astype(MXU_DTYPE)
                d = _dot(vaug_scr[g, qt + n], p)
                pv = d if pv is None else pv + d
            den_row = (1 - g) * HEAD_DIM
            den = pv[den_row:den_row + 1, :]
            if has_sink:
                col = lax.broadcasted_iota(jnp.int32, (1, rows), 1)
                sink = jnp.zeros((1, rows), F32)
                for r in range(N_REP):
                    sink = jnp.where(col >= r * tile, sink_ref[g * N_REP + r], sink)
                m2 = jnp.maximum(m, sink)
                scale = jnp.exp(m - m2)
                pv = pv * scale
                den = den * scale + jnp.exp(sink - m2)
            outs.append(pv / den)
        _emit_slabs(o_ref, row0, outs, tile)

    @pl.when(i2 == 0)
    def _():
        _build_kv(k_ref, v_ref, kaug_scr, vaug_scr, tile=tile, seq=seq, pad_tiles=n_prev, select=False)
        for idx, n in enumerate(masked):
            bias_scr[idx] = _mask_bias(tile, offs[n], window)
        _build_qaug(qaug_a, lambda r: q_ref[0, 0:tile, r * LANES:(r + 1) * LANES], 0, tile, pad_flag)
        scores_into(s_a, m_a, qaug_a, 0)

    qt = 2 * i2
    _build_qaug(qaug_b, lambda r: q_ref[0, tile:2 * tile, r * LANES:(r + 1) * LANES], (qt + 1) * tile, tile, pad_flag)
    finish(s_a, m_a, qt, 0)
    scores_into(s_b, m_b, qaug_b, qt + 1)
    qn = jnp.minimum(qt + 2, n_q - 1)
    _build_qaug(qaug_a, lambda r: qn_ref[0, :, r * LANES:(r + 1) * LANES], qn * tile, tile, pad_flag)
    scores_into(s_a, m_a, qaug_a, qn)
    finish(s_b, m_b, qt + 1, tile)


def _band_attention(p16, vt, q_blk, k_blk, v_blk, *, tile, window, sinks=None):
    b, t, _ = p16.shape
    tile = min(tile, t // 2)
    n_q = t // tile
    n_prev = -(-(window - 1) // tile)
    span = n_prev + 1
    assert n_q % 2 == 0
    has_sink = sinks is not None
    rows = N_REP * tile
    offs = [(n_prev - n) * tile for n in range(span)]
    n_masked = sum(1 for o in offs if o - (tile - 1) < 0 or o + tile - 1 >= window)
    in_specs, args = [], []
    if has_sink:
        in_specs.append(pl.BlockSpec(memory_space=pltpu.SMEM))
        args.append(sinks.astype(F32))
    in_specs += [
        pl.BlockSpec((1, 2 * tile, 512), lambda bi, i: (bi, i, q_blk)),
        pl.BlockSpec((1, tile, 512), lambda bi, i: (bi, jnp.minimum(2 * i + 2, n_q - 1), q_blk)),
        pl.BlockSpec((1, t, LANES), lambda bi, i: (bi, 0, k_blk)),
        pl.BlockSpec((LANES, t), lambda bi, i: (v_blk, bi)),
    ]
    args += [p16, p16, p16, vt]
    return pl.pallas_call(
        functools.partial(_band_kernel, tile=tile, seq=t, n_prev=n_prev, window=window, has_sink=has_sink),
        grid=(b, n_q // 2),
        in_specs=in_specs,
        out_specs=pl.BlockSpec((1, 2 * tile, 512), lambda bi, i: (bi, i, 0)),
        out_shape=jax.ShapeDtypeStruct((b, t, 512), BRANCH_DTYPE),
        scratch_shapes=[
            pltpu.VMEM((N_GROUPS, t + n_prev * tile, AUG_K), MXU_DTYPE),
            pltpu.VMEM((N_GROUPS, n_q + n_prev, LANES, tile), MXU_DTYPE),
            pltpu.VMEM((N_GROUPS, rows, AUG_K), MXU_DTYPE),
            pltpu.VMEM((N_GROUPS, rows, AUG_K), MXU_DTYPE),
            pltpu.VMEM((N_GROUPS, span * tile, rows), F32),
            pltpu.VMEM((N_GROUPS, span * tile, rows), F32),
            pltpu.VMEM((N_GROUPS, 1, rows), F32),
            pltpu.VMEM((N_GROUPS, 1, rows), F32),
            pltpu.VMEM((n_masked, tile, rows), F32),
        ],
        compiler_params=_params("arbitrary", "arbitrary"),
        name="attn_band" + ("_sink" if has_sink else ""),
    )(*args)


def _slab_cols(w):
    lead = w.shape[:-1]
    return w.reshape(*lead, N_GROUPS, N_REP, HEAD_DIM).swapaxes(-3, -2).reshape(*lead, N_GROUPS * N_REP * HEAD_DIM)


def _arrange_proj_weight(w):
    q_a = _slab_cols(w[:, 0:512])
    kc, vc, ks, vs, kw, vw = [w[:, 512 + j * LANES:512 + (j + 1) * LANES] for j in range(6)]
    gates = w[:, 1280:1304].reshape(-1, N_GROUPS, N_REP, N_BRANCH).transpose(0, 3, 2, 1).reshape(-1, 24)
    gates = jnp.pad(gates, ((0, 0), (0, LANES - 24)))
    q_b = _slab_cols(w[:, 1304:1816])
    k_b, v_b = w[:, 1816:1944], w[:, 1944:2072]
    main = jnp.concatenate([q_a, q_b, ks, kw, k_b, kc, vc, gates], axis=1).astype(MXU_DTYPE)
    values_t = jnp.concatenate([vs, vw, v_b], axis=1).T.astype(MXU_DTYPE)
    return main, values_t


def _arrange_out_weight(w):
    w_a = _slab_cols(w[0:512].T).T
    w_b = _slab_cols(w[512:1024].T).T
    return jnp.concatenate([w_a, w_b], axis=0).astype(MXU_DTYPE)


_QA_BLK, _QB_BLK = 0, 1
_KS_BLK, _KW_BLK, _KB_BLK = 8, 9, 10
_VS_BLK, _VW_BLK, _VB_BLK = 0, 1, 2


def kernel(x, ffn1_norm, ffn1_w_in, ffn1_w_out, mix_norm, w_mix_in, cmp_k_pos, cmp_k_w1, cmp_k_b1, cmp_k_w2, cmp_v_pos, cmp_v_w1, cmp_v_b1, cmp_v_w2, swa_sinks, w_mix_out, ffn2_norm, ffn2_w_in, ffn2_w_out, final_norm):
    b, t, d = x.shape
    m = b * t
    depth = ffn1_norm.shape[0]
    h = x.reshape(m, d)
    for l in range(depth):
        last = l == depth - 1
        h, p16, kc_raw, vc_raw, gates, vt = _ffn_proj(
            h, ffn1_norm[l], ffn1_w_in[l].astype(MXU_DTYPE), ffn1_w_out[l].astype(MXU_DTYPE),
            mix_norm[l], *_arrange_proj_weight(w_mix_in[l]))
        p16 = p16.reshape(b, t, P16_W)
        kc = _compress(kc_raw.reshape(b, t, LANES), cmp_k_pos[l], cmp_k_w1[l], cmp_k_b1[l], cmp_k_w2[l])
        vc = _compress(vc_raw.reshape(b, t, LANES), cmp_v_pos[l], cmp_v_w1[l], cmp_v_b1[l], cmp_v_w2[l])
        o_cmp, notsel, block_need = _cmp_attention(p16, kc, vc)
        o_slc = _sel_attention(p16, vt, notsel, block_need, _QA_BLK, _KS_BLK, _VS_BLK, tile=SEL_TILE)
        o_win = _band_attention(p16, vt, _QA_BLK, _KW_BLK, _VW_BLK, tile=256, window=NSA_WINDOW)
        o_b = _band_attention(p16, vt, _QB_BLK, _KB_BLK, _VB_BLK, tile=128, window=SWA_WINDOW, sinks=swa_sinks[l])
        h = _combine_ffn(
            h, o_cmp.reshape(m, 512), o_slc.reshape(m, 512), o_win.reshape(m, 512), o_b.reshape(m, 512), gates,
            _arrange_out_weight(w_mix_out[l]), ffn2_norm[l], ffn2_w_in[l].astype(MXU_DTYPE),
            ffn2_w_out[l].astype(MXU_DTYPE), final_norm if last else None)
    if depth == 0:
        raise ValueError("depth must be positive")
    return h.reshape(b, t, d)
```
